```python
import math
import jax, jax.numpy as jnp
from jax import lax
import numpy as np

D_MODEL = 1024
BATCH = 8
SEQ = 2048
DEPTH = 1

N_ATTN_HEADS = 8
HEAD_DIM = 64
D_ATTN = N_ATTN_HEADS * HEAD_DIM
MOBA_BLOCK = 256
MOBA_TOPK = 3
MOBA_Q_CHUNK = 32
N_SSM_HEADS = 16
SSM_HEAD_DIM = 64
D_SSM = N_SSM_HEADS * SSM_HEAD_DIM
N_SSM_GROUPS = 2
D_STATE = 128
SSM_CONV = 4
SSD_CHUNK = 256
D_XBC = D_SSM + 2 * N_SSM_GROUPS * D_STATE
D_MIX = D_ATTN + D_SSM
D_IN_PROJ = 3 * D_ATTN + D_SSM + D_XBC + N_SSM_HEADS
D_FF = 2816
FFN_CONV = 3
DEEPNORM_ALPHA = (2.0 * DEPTH) ** 0.25
DEEPNORM_BETA = (8.0 * DEPTH) ** -0.25
LN_EPS = 1e-5
RMS_EPS = 1e-5
PAD_MULT = max(MOBA_BLOCK, SSD_CHUNK)

kernel_name = "hymba_moba_ssd_convffn_deepnorm_adaln"


def layer_norm(x, g, b):
    xf = x.astype(jnp.float32)
    mu = jnp.mean(xf, -1, keepdims=True)
    var = jnp.mean(jnp.square(xf - mu), -1, keepdims=True)
    return ((xf - mu) * lax.rsqrt(var + LN_EPS) * g + b).astype(x.dtype)


def causal_dwconv(x, w, b):
    k = w.shape[0]
    y = lax.conv_general_dilated(x, w[:, None, :].astype(x.dtype), window_strides=(1,),
                                 padding=[(k - 1, 0)], dimension_numbers=('NWC', 'WIO', 'NWC'),
                                 feature_group_count=x.shape[-1])
    return y + b.astype(x.dtype)


def alibi_slopes(n):
    return jnp.asarray(2.0 ** (-8.0 * np.arange(1, n + 1) / n), dtype=jnp.float32)


def moba_attention(q, k, v):
    b, h, sp, dh = q.shape
    nb = sp // MOBA_BLOCK
    ksel = min(MOBA_TOPK, nb)
    scale = dh ** -0.5
    slopes = alibi_slopes(h)
    kb = k.reshape(b, h, nb, MOBA_BLOCK, dh)
    vb = v.reshape(b, h, nb, MOBA_BLOCK, dh)
    k_mean = jnp.mean(kb.astype(jnp.float32), axis=3)
    q_blk = jnp.arange(sp) // MOBA_BLOCK
    gate = jnp.einsum('bhsd,bhnd->bhsn', q.astype(jnp.float32), k_mean)
    past = jnp.arange(nb)[None, :] < q_blk[:, None]
    gate = jnp.where(past, gate, -jnp.inf)
    _, idx = lax.top_k(gate, ksel)
    valid = idx < q_blk[:, None]

    nc = sp // MOBA_Q_CHUNK

    def to_chunks(a):
        a = a.reshape((b, h, nc, MOBA_Q_CHUNK) + a.shape[3:])
        return jnp.moveaxis(a, 2, 0)

    bi = jnp.arange(b)[:, None, None, None]
    hi = jnp.arange(h)[None, :, None, None]
    off = jnp.arange(MOBA_BLOCK)

    def chunk_attn(args):
        qc, idxc, validc, ci = args
        t = ci * MOBA_Q_CHUNK + jnp.arange(MOBA_Q_CHUNK)
        own = (ci * MOBA_Q_CHUNK) // MOBA_BLOCK
        kg = kb[bi, hi, idxc]
        vg = vb[bi, hi, idxc]
        s_sel = idxc[..., None] * MOBA_BLOCK + off
        sc_sel = (jnp.einsum('bhqd,bhqjkd->bhqjk', qc, kg).astype(jnp.float32) * scale
                  - slopes[:, None, None, None] * (t[:, None, None] - s_sel))
        sc_sel = jnp.where(validc[..., None], sc_sel, -jnp.inf)
        k_own = lax.dynamic_index_in_dim(kb, own, axis=2, keepdims=False)
        v_own = lax.dynamic_index_in_dim(vb, own, axis=2, keepdims=False)
        s_own = own * MOBA_BLOCK + off
        dist = t[:, None] - s_own[None, :]
        sc_own = (jnp.einsum('bhqd,bhkd->bhqk', qc, k_own).astype(jnp.float32) * scale
                  - slopes[:, None, None] * dist)
        sc_own = jnp.where(dist >= 0, sc_own, -jnp.inf)
        sc = jnp.concatenate([sc_sel.reshape(b, h, MOBA_Q_CHUNK, ksel * MOBA_BLOCK), sc_own], -1)
        p = jax.nn.softmax(sc, axis=-1).astype(v.dtype)
        p_sel = p[..., :ksel * MOBA_BLOCK].reshape(b, h, MOBA_Q_CHUNK, ksel, MOBA_BLOCK)
        p_own = p[..., ksel * MOBA_BLOCK:]
        return (jnp.einsum('bhqjk,bhqjkd->bhqd', p_sel, vg)
                + jnp.einsum('bhqk,bhkd->bhqd', p_own, v_own))

    out = lax.map(chunk_attn, (to_chunks(q), to_chunks(idx), to_chunks(valid), jnp.arange(nc)))
    return jnp.moveaxis(out, 0, 2).reshape(b, h, sp, dh)


def segsum_exp(a):
    cs = jnp.cumsum(a, -1)
    diff = cs[..., :, None] - cs[..., None, :]
    n = a.shape[-1]
    mask = jnp.tril(jnp.ones((n, n), dtype=bool))
    return jnp.exp(jnp.where(mask, diff, -jnp.inf))


def ssd_scan(x, dt, a, bmat, cmat):
    b, sp, h, p = x.shape
    n = bmat.shape[-1]
    nc = sp // SSD_CHUNK
    xc = (x.astype(jnp.float32) * dt[..., None]).reshape(b, nc, SSD_CHUNK, h, p)
    bc = bmat.astype(jnp.float32).reshape(b, nc, SSD_CHUNK, h, n)
    cc = cmat.astype(jnp.float32).reshape(b, nc, SSD_CHUNK, h, n)
    dac = jnp.moveaxis((dt * a).reshape(b, nc, SSD_CHUNK, h), -1, 1)
    da_cs = jnp.cumsum(dac, -1)
    lmat = segsum_exp(dac)
    scores = jnp.einsum('bclhn,bcshn->bhcls', cc, bc) * lmat
    y_diag = jnp.einsum('bhcls,bcshp->bclhp', scores, xc)
    decay = jnp.exp(da_cs[..., -1:] - da_cs)
    states = jnp.einsum('bclhn,bhcl,bclhp->bchpn', bc, decay, xc)
    chunk_decay = jnp.exp(da_cs[..., -1])

    def step(carry, inp):
        st, dec = inp
        return carry * dec[..., None, None] + st, carry

    init = jnp.zeros((b, h, p, n), jnp.float32)
    _, prev = lax.scan(step, init, (jnp.moveaxis(states, 1, 0), jnp.moveaxis(chunk_decay, 2, 0)))
    prev = jnp.moveaxis(prev, 0, 1)
    y_off = jnp.einsum('bclhn,bchpn,bhcl->bclhp', cc, prev, jnp.exp(da_cs))
    return (y_diag + y_off).reshape(b, sp, h, p)


def hybrid_mixer(h, w_in, conv_w, conv_b, dt_bias, a_log, d_skip, norm_w, w_out):
    b, s, _ = h.shape
    sp = -(-s // PAD_MULT) * PAD_MULT
    proj = jnp.pad(h @ w_in, ((0, 0), (0, sp - s), (0, 0)))
    q, k, v, z, xbc, dt_raw = jnp.split(
        proj, [D_ATTN, 2 * D_ATTN, 3 * D_ATTN, 3 * D_ATTN + D_SSM, 3 * D_ATTN + D_SSM + D_XBC], axis=-1)

    def heads(t):
        return t.reshape(b, sp, N_ATTN_HEADS, HEAD_DIM).transpose(0, 2, 1, 3)
    o_attn = moba_attention(heads(q), heads(k), heads(v))
    o_attn = o_attn.transpose(0, 2, 1, 3).reshape(b, sp, D_ATTN)

    xbc = jax.nn.silu(causal_dwconv(xbc, conv_w, conv_b))
    xs, bm, cm = jnp.split(xbc, [D_SSM, D_SSM + N_SSM_GROUPS * D_STATE], axis=-1)
    xs = xs.reshape(b, sp, N_SSM_HEADS, SSM_HEAD_DIM)
    rep = N_SSM_HEADS // N_SSM_GROUPS
    bm = jnp.repeat(bm.reshape(b, sp, N_SSM_GROUPS, D_STATE), rep, axis=2)
    cm = jnp.repeat(cm.reshape(b, sp, N_SSM_GROUPS, D_STATE), rep, axis=2)
    dt = jax.nn.softplus(dt_raw.astype(jnp.float32) + dt_bias)
    a = -jnp.exp(a_log.astype(jnp.float32))
    y = ssd_scan(xs, dt, a, bm, cm) + xs.astype(jnp.float32) * d_skip[:, None]
    y = y.reshape(b, sp, D_SSM) * jax.nn.silu(z.astype(jnp.float32))
    yg = y.reshape(b, sp, N_SSM_GROUPS, D_SSM // N_SSM_GROUPS)
    yg = yg * lax.rsqrt(jnp.mean(jnp.square(yg), -1, keepdims=True) + RMS_EPS)
    y = yg.reshape(b, sp, D_SSM) * norm_w

    o = jnp.concatenate([o_attn, y.astype(o_attn.dtype)], axis=-1)[:, :s]
    return o @ w_out


def conv_ffn(h, w_up, conv_w, conv_b, w_down):
    u = causal_dwconv(h @ w_up, conv_w, conv_b)
    g, val = jnp.split(u, 2, axis=-1)
    return (jax.nn.silu(g) * val) @ w_down


def setup_inputs(seed: int = 0) -> dict:
    key = jax.random.key(seed)
    ks = jax.random.split(key, 24)
    f32 = jnp.float32
    nrm = lambda k, shape, std: jax.random.normal(k, shape, f32) * std
    dt0 = jnp.exp(jax.random.uniform(ks[8], (DEPTH, N_SSM_HEADS), f32, math.log(1e-3), math.log(1e-1)))
    return {
        "x": nrm(ks[0], (BATCH, SEQ, D_MODEL), 1.0),
        "c": nrm(ks[1], (BATCH, D_MODEL), 1.0),
        "ada_w": nrm(ks[2], (DEPTH, D_MODEL, 6 * D_MODEL), 0.1 * D_MODEL ** -0.5),
        "ada_b": nrm(ks[3], (DEPTH, 6 * D_MODEL), 0.01),
        "mix_in_w": nrm(ks[4], (DEPTH, D_MODEL, D_IN_PROJ), D_MODEL ** -0.5),
        "ssm_conv_w": nrm(ks[5], (DEPTH, SSM_CONV, D_XBC), SSM_CONV ** -0.5),
        "ssm_conv_b": nrm(ks[6], (DEPTH, D_XBC), 0.01),
        "ssm_dt_bias": dt0 + jnp.log(-jnp.expm1(-dt0)),
        "ssm_a_log": jnp.log(jax.random.uniform(ks[9], (DEPTH, N_SSM_HEADS), f32, 1.0, 16.0)),
        "ssm_d": 1.0 + nrm(ks[10], (DEPTH, N_SSM_HEADS), 0.1),
        "ssm_norm_w": 1.0 + nrm(ks[11], (DEPTH, D_SSM), 0.02),
        "mix_out_w": nrm(ks[12], (DEPTH, D_MIX, D_MODEL), DEEPNORM_BETA * math.sqrt(2.0 / (D_MIX + D_MODEL))),
        "ln1_g": 1.0 + nrm(ks[13], (DEPTH, D_MODEL), 0.02),
        "ln1_b": nrm(ks[14], (DEPTH, D_MODEL), 0.01),
        "ffn_up_w": nrm(ks[15], (DEPTH, D_MODEL, 2 * D_FF), D_MODEL ** -0.5),
        "ffn_conv_w": nrm(ks[16], (DEPTH, FFN_CONV, 2 * D_FF), FFN_CONV ** -0.5),
        "ffn_conv_b": nrm(ks[17], (DEPTH, 2 * D_FF), 0.01),
        "ffn_down_w": nrm(ks[18], (DEPTH, D_FF, D_MODEL), DEEPNORM_BETA * math.sqrt(2.0 / (D_FF + D_MODEL))),
        "ln2_g": 1.0 + nrm(ks[19], (DEPTH, D_MODEL), 0.02),
        "ln2_b": nrm(ks[20], (DEPTH, D_MODEL), 0.01),
    }


def reference(x, c, ada_w, ada_b, mix_in_w, ssm_conv_w, ssm_conv_b, ssm_dt_bias, ssm_a_log,
              ssm_d, ssm_norm_w, mix_out_w, ln1_g, ln1_b, ffn_up_w, ffn_conv_w, ffn_conv_b,
              ffn_down_w, ln2_g, ln2_b):
    c_act = jax.nn.silu(c)
    for i in range(DEPTH):
        mod = (c_act @ ada_w[i] + ada_b[i])[:, None, :]
        sh1, sc1, g1, sh2, sc2, g2 = jnp.split(mod, 6, axis=-1)
        h = x * (1.0 + sc1) + sh1
        y = hybrid_mixer(h, mix_in_w[i], ssm_conv_w[i], ssm_conv_b[i], ssm_dt_bias[i], ssm_a_log[i],
                         ssm_d[i], ssm_norm_w[i], mix_out_w[i])
        x = layer_norm(DEEPNORM_ALPHA * x + (1.0 + g1) * y, ln1_g[i], ln1_b[i])
        h = x * (1.0 + sc2) + sh2
        y = conv_ffn(h, ffn_up_w[i], ffn_conv_w[i], ffn_conv_b[i], ffn_down_w[i])
        x = layer_norm(DEEPNORM_ALPHA * x + (1.0 + g2) * y, ln2_g[i], ln2_b[i])
    return x
```

```python
import functools

import jax
import jax.numpy as jnp
from jax import lax
from jax.experimental import pallas as pl
from jax.experimental.pallas import tpu as pltpu

F32 = jnp.float32
BF16 = jnp.bfloat16

N_ATTN_HEADS = 8
HEAD_DIM = 64
MOBA_BLOCK = 256
MOBA_TOPK = 3
N_SSM_HEADS = 16
SSM_HEAD_DIM = 64
N_SSM_GROUPS = 2
D_STATE = 128
SSD_CHUNK = 256
LN_EPS = 1e-5
RMS_EPS = 1e-5

LANES = 128
HALO = 8
NEG_BIG = -1e30
VMEM_LIMIT = 52 * 1024 * 1024


def _cparams(sem):
    return pltpu.CompilerParams(dimension_semantics=sem, vmem_limit_bytes=VMEM_LIMIT)


def _sigmoid(x):
    return 1.0 / (1.0 + jnp.exp(-x))


def _dot(a, b):
    return jnp.dot(a, b, preferred_element_type=F32)


def _dot_nt(a, b):
    return lax.dot_general(a, b, (((1,), (1,)), ((), ())), preferred_element_type=F32)


def _layer_norm(r, g, b):
    mu = jnp.mean(r, axis=-1, keepdims=True)
    d = r - mu
    var = jnp.mean(d * d, axis=-1, keepdims=True)
    return d * lax.rsqrt(var + LN_EPS) * g + b


def _adaln_kernel(c_ref, w_ref, b_ref, o_ref):
    c = c_ref[...]
    ca = (c * _sigmoid(c)).astype(BF16)
    o_ref[...] = _dot(ca, w_ref[...].astype(BF16)) + b_ref[...]


def _adaln(c, w, b):
    bsz, d = c.shape
    n = w.shape[1]
    tn = d
    return pl.pallas_call(
        _adaln_kernel,
        grid=(n // tn,),
        in_specs=[pl.BlockSpec((bsz, d), lambda j: (0, 0)),
                  pl.BlockSpec((d, tn), lambda j: (0, j)),
                  pl.BlockSpec((1, tn), lambda j: (0, j))],
        out_specs=pl.BlockSpec((bsz, tn), lambda j: (0, j)),
        out_shape=jax.ShapeDtypeStruct((bsz, n), F32),
        compiler_params=_cparams(("parallel",)),
        name="adaln",
    )(c, w, b.reshape(1, n))


def _inproj_kernel(x_ref, mod_ref, w_ref, qkv_ref, z_ref, xbc_ref, dt_ref, h_scr, *, segs, chunk):
    sh = mod_ref[0, 0:1, :]
    sc = mod_ref[0, 1:2, :]
    h_scr[...] = (x_ref[0] * (1.0 + sc) + sh).astype(BF16)
    outs = (qkv_ref, z_ref, xbc_ref, dt_ref)
    for dst, (c0, width) in zip(outs, segs):
        for a in range(0, width, chunk):
            n = min(chunk, width - a)
            dst[0, :, a:a + n] = _dot(h_scr[...], w_ref[:, c0 + a:c0 + a + n]).astype(dst.dtype)


def _inproj(x, mod3, w_pad, segs, tm=512, chunk=512):
    bsz, s, d = x.shape
    ntot = w_pad.shape[1]
    widths = [w for _, w in segs]
    dts = [BF16, F32, F32, F32]
    kern = functools.partial(_inproj_kernel, segs=segs, chunk=chunk)
    return pl.pallas_call(
        kern,
        grid=(bsz, s // tm),
        in_specs=[pl.BlockSpec((1, tm, d), lambda b, i: (b, i, 0)),
                  pl.BlockSpec((1, 6, d), lambda b, i: (b, 0, 0)),
                  pl.BlockSpec((d, ntot), lambda b, i: (0, 0), pipeline_mode=pl.Buffered(1))],
        out_specs=[pl.BlockSpec((1, tm, w), lambda b, i: (b, i, 0)) for w in widths],
        out_shape=[jax.ShapeDtypeStruct((bsz, s, w), dt) for w, dt in zip(widths, dts)],
        scratch_shapes=[pltpu.VMEM((tm, d), BF16)],
        compiler_params=_cparams(("parallel", "parallel")),
        name="inproj",
    )(x, mod3, w_pad)


def _attn_kernel(q_ref, k_ref, v_ref, slope_ref, o_ref, kmean_scr, m_scr, l_scr, acc_scr, *, nb, scale):
    blk = MOBA_BLOCK
    i = pl.program_id(2)

    @pl.when(i == 0)
    def _():
        kmean_scr[...] = jnp.zeros_like(kmean_scr)
        for j in range(nb):
            kj = k_ref[0, j * blk:(j + 1) * blk, :].astype(F32)
            kmean_scr[j:j + 1, :] = jnp.mean(kj, axis=0, keepdims=True)

    q = q_ref[0]
    lane = lax.broadcasted_iota(jnp.int32, (blk, LANES), 1)
    row = lax.broadcasted_iota(jnp.int32, (blk, blk), 0)
    col = lax.broadcasted_iota(jnp.int32, (blk, blk), 1)
    rel = (row - col).astype(F32)
    kmean = kmean_scr[...].astype(BF16)
    k_own = k_ref[0, pl.ds(pl.multiple_of(i * blk, blk), blk), :]
    v_own = v_ref[0, pl.ds(pl.multiple_of(i * blk, blk), blk), :]

    outs = []
    for hh in range(2):
        head_lanes = (lane // HEAD_DIM) == hh
        qh = jnp.where(head_lanes, q, jnp.zeros_like(q))
        slope = slope_ref[0, hh:hh + 1, :]
        slope_rel = slope * rel

        gate = _dot_nt(qh, kmean)
        gate = jnp.where(lane < i, gate, -jnp.inf)
        rank = jnp.zeros((blk, LANES), F32)
        for jp in range(nb):
            gj = gate[:, jp:jp + 1]
            beats = (gj > gate) | ((gj == gate) & (lane > jp))
            rank = rank + jnp.where(beats, 1.0, 0.0)
        sel_bias = jnp.where((rank < MOBA_TOPK) & (lane < i), 0.0, -jnp.inf)

        s = _dot_nt(qh, k_own) * scale - slope_rel
        s = jnp.where(rel >= 0, s, -jnp.inf)
        m0 = jnp.max(s, axis=-1, keepdims=True)
        p = jnp.exp(s - m0)
        m_scr[hh] = m0
        l_scr[hh] = jnp.sum(p, axis=-1, keepdims=True)
        acc_scr[hh] = _dot(p.astype(BF16), v_own)

        for j in range(nb - 1):
            @pl.when(j < i)
            def _(j=j, hh=hh, qh=qh, slope=slope, slope_rel=slope_rel, sel_bias=sel_bias):
                kj = k_ref[0, j * blk:(j + 1) * blk, :]
                vj = v_ref[0, j * blk:(j + 1) * blk, :]
                dist0 = ((i - j) * blk).astype(F32)
                bias = sel_bias[:, j:j + 1] - slope[:, 0:1] * dist0
                sj = _dot_nt(qh, kj) * scale - slope_rel + bias
                m_old = m_scr[hh]
                m_new = jnp.maximum(m_old, jnp.max(sj, axis=-1, keepdims=True))
                alpha = jnp.exp(m_old - m_new)
                pj = jnp.exp(sj - m_new)
                m_scr[hh] = m_new
                l_scr[hh] = alpha * l_scr[hh] + jnp.sum(pj, axis=-1, keepdims=True)
                acc_scr[hh] = alpha * acc_scr[hh] + _dot(pj.astype(BF16), vj)

        outs.append(acc_scr[hh] / l_scr[hh])

    o_ref[0] = jnp.where((lane // HEAD_DIM) == 0, outs[0], outs[1]).astype(o_ref.dtype)


def _attention(qkv, slopes):
    bsz, s, _ = qkv.shape
    blk = MOBA_BLOCK
    nb = s // blk
    npairs = N_ATTN_HEADS // 2
    kern = functools.partial(_attn_kernel, nb=nb, scale=HEAD_DIM ** -0.5)
    return pl.pallas_call(
        kern,
        grid=(bsz, npairs, nb),
        in_specs=[pl.BlockSpec((1, blk, LANES), lambda b, hp, i: (b, i, hp)),
                  pl.BlockSpec((1, s, LANES), lambda b, hp, i: (b, 0, npairs + hp)),
                  pl.BlockSpec((1, s, LANES), lambda b, hp, i: (b, 0, 2 * npairs + hp)),
                  pl.BlockSpec((1, 2, blk), lambda b, hp, i: (hp, 0, 0))],
        out_specs=pl.BlockSpec((1, blk, LANES), lambda b, hp, i: (b, i, hp)),
        out_shape=jax.ShapeDtypeStruct((bsz, s, npairs * LANES), BF16),
        scratch_shapes=[pltpu.VMEM((LANES, LANES), F32),
                        pltpu.VMEM((2, blk, 1), F32),
                        pltpu.VMEM((2, blk, 1), F32),
                        pltpu.VMEM((2, blk, LANES), F32)],
        compiler_params=_cparams(("parallel", "parallel", "arbitrary")),
        name="moba_attn",
    )(qkv, qkv, qkv, slopes)


def _conv_silu(x_ref, halo_ref, ext_ref, w_ref, b_ref, first):
    rows = x_ref.shape[1]
    taps = w_ref.shape[0]

    @pl.when(first)
    def _():
        halo_ref[...] = jnp.zeros_like(halo_ref)

    ext_ref[0:HALO, :] = halo_ref[...]
    ext_ref[HALO:HALO + rows, :] = x_ref[0]
    halo_ref[...] = x_ref[0, rows - HALO:rows, :]
    acc = b_ref[...] + w_ref[taps - 1:taps, :] * ext_ref[HALO:HALO + rows, :]
    for j in range(taps - 1):
        off = HALO - (taps - 1) + j
        acc = acc + w_ref[j:j + 1, :] * ext_ref[off:off + rows, :]
    return acc * _sigmoid(acc)


def _ssd_kernel(xs_ref, bm_ref, cm_ref, dt_ref, z_ref,
                wx_ref, wb_ref, wc_ref, bx_ref, bb_ref, bc_ref,
                dtb_ref, alog_ref, dskip_ref, nw_ref,
                y_ref,
                hx_scr, hb_scr, hc_scr, ex_scr, eb_scr, ec_scr, state_scr, xdec_scr, ydiag_scr):
    L = SSD_CHUNK
    hpg = N_SSM_HEADS // N_SSM_GROUPS
    first = pl.program_id(2) == 0

    xs = _conv_silu(xs_ref, hx_scr, ex_scr, wx_ref, bx_ref, first)
    bm = _conv_silu(bm_ref, hb_scr, eb_scr, wb_ref, bb_ref, first)
    cm = _conv_silu(cm_ref, hc_scr, ec_scr, wc_ref, bc_ref, first)

    @pl.when(first)
    def _():
        state_scr[...] = jnp.zeros_like(state_scr)

    xdt_in = dt_ref[0] + dtb_ref[...]
    dt = jnp.maximum(xdt_in, 0.0) + jnp.log1p(jnp.exp(-jnp.abs(xdt_in)))
    a = -jnp.exp(alog_ref[...])
    da = dt * a
    row = lax.broadcasted_iota(jnp.int32, (L, L), 0)
    col = lax.broadcasted_iota(jnp.int32, (L, L), 1)
    causal = row >= col
    tri = jnp.where(causal, 1.0, 0.0).astype(F32)
    tri_t = jnp.where(row <= col, 1.0, 0.0).astype(F32)
    cs = jnp.dot(tri, da, preferred_element_type=F32, precision=lax.Precision.HIGHEST)
    cs_t = jnp.dot(da.T, tri_t, preferred_element_type=F32, precision=lax.Precision.HIGHEST)
    cs_last = cs[L - 1:L, :]

    bm16 = bm.astype(BF16)
    cm16 = cm.astype(BF16)
    cb = _dot_nt(cm16, bm16)
    lane = lax.broadcasted_iota(jnp.int32, (L, LANES), 1)
    lo_half = lane < SSM_HEAD_DIM

    cd_parts, eo_parts, xdt_parts = [], [], []
    for pr in range(hpg // 2):
        h0, h1 = 2 * pr, 2 * pr + 1
        xs_p = xs[:, pr * LANES:(pr + 1) * LANES]
        cs_p = jnp.where(lo_half, cs[:, h0:h0 + 1], cs[:, h1:h1 + 1])
        dt_p = jnp.where(lo_half, dt[:, h0:h0 + 1], dt[:, h1:h1 + 1])
        csl_p = jnp.where(lo_half[0:1], cs_last[:, h0:h0 + 1], cs_last[:, h1:h1 + 1])
        xdt_p = xs_p * dt_p
        xdt16 = xdt_p.astype(BF16)
        xdec_scr[:, pr * LANES:(pr + 1) * LANES] = (xdt_p * jnp.exp(csl_p - cs_p)).astype(BF16)
        cd_parts.append(jnp.exp(csl_p))
        eo_parts.append(jnp.exp(cs_p))
        ys = []
        for h in (h0, h1):
            diff = cs[:, h:h + 1] - cs_t[h:h + 1, :]
            lmat = jnp.exp(jnp.where(causal, diff, -jnp.inf))
            ys.append(_dot((cb * lmat).astype(BF16), xdt16))
        ydiag_scr[:, pr * LANES:(pr + 1) * LANES] = jnp.where(lo_half, ys[0], ys[1])

    eo = jnp.concatenate(eo_parts, axis=-1)
    cd = jnp.concatenate(cd_parts, axis=-1)
    state = state_scr[...]
    y = ydiag_scr[...] + eo * _dot(cm16, state.astype(BF16)) + xs * dskip_ref[...]
    state_scr[...] = state * cd + _dot(bm.T.astype(BF16), xdec_scr[...])

    z = z_ref[0]
    y = y * (z * _sigmoid(z))
    y = y * lax.rsqrt(jnp.mean(y * y, axis=-1, keepdims=True) + RMS_EPS) * nw_ref[...]
    y_ref[0] = y.astype(y_ref.dtype)


def _ssd(xbc, dt, z, conv_w, conv_b, dtb, alog, dskip, norm_w):
    bsz, s, _ = xbc.shape
    L = SSD_CHUNK
    nc = s // L
    G = N_SSM_GROUPS
    gw = (N_SSM_HEADS // G) * SSM_HEAD_DIM
    d_ssm = N_SSM_HEADS * SSM_HEAD_DIM
    taps = conv_w.shape[0]
    xo = gw // LANES
    bo = d_ssm // LANES
    co = bo + G * D_STATE // LANES

    def seq(width, off_fn):
        return pl.BlockSpec((1, L, width), lambda b, g, c: (b, c, off_fn(g)))

    def par(rows, width, off_fn):
        return pl.BlockSpec((rows, width), lambda b, g, c: (0, off_fn(g)))

    in_specs = [
        seq(gw, lambda g: g), seq(D_STATE, lambda g: bo + g), seq(D_STATE, lambda g: co + g),
        seq(LANES, lambda g: g), seq(gw, lambda g: g),
        par(taps, gw, lambda g: g), par(taps, D_STATE, lambda g: bo + g), par(taps, D_STATE, lambda g: co + g),
        par(1, gw, lambda g: g), par(1, D_STATE, lambda g: bo + g), par(1, D_STATE, lambda g: co + g),
        par(1, LANES, lambda g: g), par(1, LANES, lambda g: g),
        par(1, gw, lambda g: g), par(1, gw, lambda g: g),
    ]
    return pl.pallas_call(
        _ssd_kernel,
        grid=(bsz, G, nc),
        in_specs=in_specs,
        out_specs=pl.BlockSpec((1, L, gw), lambda b, g, c: (b, c, g)),
        out_shape=jax.ShapeDtypeStruct((bsz, s, d_ssm), BF16),
        scratch_shapes=[pltpu.VMEM((HALO, gw), F32), pltpu.VMEM((HALO, D_STATE), F32), pltpu.VMEM((HALO, D_STATE), F32),
                        pltpu.VMEM((HALO + L, gw), F32), pltpu.VMEM((HALO + L, D_STATE), F32),
                        pltpu.VMEM((HALO + L, D_STATE), F32),
                        pltpu.VMEM((D_STATE, gw), F32), pltpu.VMEM((L, gw), BF16), pltpu.VMEM((L, gw), F32)],
        compiler_params=_cparams(("parallel", "parallel", "arbitrary")),
        name="ssd",
    )(xbc, xbc, xbc, dt, z, conv_w, conv_w, conv_w, conv_b, conv_b, conv_b, dtb, alog, dskip, norm_w)


def _outproj_kernel(oa_ref, y_ref, x_ref, mod_ref, w_ref, g_ref, b_ref, o_ref, *, alpha, d_attn):
    acc = _dot(oa_ref[0], w_ref[0:d_attn, :]) + _dot(y_ref[0], w_ref[d_attn:, :])
    gate = mod_ref[0, 2:3, :]
    r = alpha * x_ref[0] + (1.0 + gate) * acc
    o_ref[0] = _layer_norm(r, g_ref[...], b_ref[...])


def _outproj(oa, y, x, mod3, w, g, b, alpha, tm=512):
    bsz, s, d = x.shape
    d_attn = oa.shape[-1]
    d_ssm = y.shape[-1]
    kern = functools.partial(_outproj_kernel, alpha=alpha, d_attn=d_attn)
    return pl.pallas_call(
        kern,
        grid=(bsz, s // tm),
        in_specs=[pl.BlockSpec((1, tm, d_attn), lambda b, i: (b, i, 0)),
                  pl.BlockSpec((1, tm, d_ssm), lambda b, i: (b, i, 0)),
                  pl.BlockSpec((1, tm, d), lambda b, i: (b, i, 0)),
                  pl.BlockSpec((1, 6, d), lambda b, i: (b, 0, 0)),
                  pl.BlockSpec((d_attn + d_ssm, d), lambda b, i: (0, 0), pipeline_mode=pl.Buffered(1)),
                  pl.BlockSpec((1, d), lambda b, i: (0, 0)),
                  pl.BlockSpec((1, d), lambda b, i: (0, 0))],
        out_specs=pl.BlockSpec((1, tm, d), lambda b, i: (b, i, 0)),
        out_shape=jax.ShapeDtypeStruct((bsz, s, d), F32),
        compiler_params=_cparams(("parallel", "parallel")),
        name="outproj_ln",
    )(oa, y, x, mod3, w, g.reshape(1, d), b.reshape(1, d))


def _ffn_kernel(x_ref, mod_ref, wu_ref, cw_ref, cb_ref, wd_ref, g_ref, b_ref, o_ref,
                h_scr, halo_scr, ext_scr, acc_scr, *, alpha, d_ff, chunk):
    tm = x_ref.shape[1]
    taps = cw_ref.shape[0]
    first = pl.program_id(1) == 0

    @pl.when(first)
    def _():
        halo_scr[...] = jnp.zeros_like(halo_scr)

    x = x_ref[0]
    sh = mod_ref[0, 3:4, :]
    sc = mod_ref[0, 4:5, :]
    h_scr[...] = (x * (1.0 + sc) + sh).astype(BF16)

    def conv_cols(c0):
        u = _dot(h_scr[...], wu_ref[:, c0:c0 + chunk])
        ext_scr[0:HALO, :] = halo_scr[:, c0:c0 + chunk]
        ext_scr[HALO:HALO + tm, :] = u
        halo_scr[:, c0:c0 + chunk] = u[tm - HALO:tm, :]
        out = cb_ref[:, c0:c0 + chunk] + cw_ref[taps - 1:taps, c0:c0 + chunk] * u
        for j in range(taps - 1):
            off = HALO - (taps - 1) + j
            out = out + cw_ref[j:j + 1, c0:c0 + chunk] * ext_scr[off:off + tm, :]
        return out

    for ci, c0 in enumerate(range(0, d_ff, chunk)):
        gte = conv_cols(c0)
        val = conv_cols(d_ff + c0)
        act = (gte * _sigmoid(gte) * val).astype(BF16)
        part = _dot(act, wd_ref[c0:c0 + chunk, :])
        if ci == 0:
            acc_scr[...] = part
        else:
            acc_scr[...] += part

    gate = mod_ref[0, 5:6, :]
    r = alpha * x + (1.0 + gate) * acc_scr[...]
    o_ref[0] = _layer_norm(r, g_ref[...], b_ref[...])


def _ffn(x, mod3, w_up, conv_w, conv_b, w_down, g, b, alpha, tm=256, chunk=256):
    bsz, s, d = x.shape
    d_ff = w_down.shape[0]
    taps = conv_w.shape[0]
    kern = functools.partial(_ffn_kernel, alpha=alpha, d_ff=d_ff, chunk=chunk)
    return pl.pallas_call(
        kern,
        grid=(bsz, s // tm),
        in_specs=[pl.BlockSpec((1, tm, d), lambda b, i: (b, i, 0)),
                  pl.BlockSpec((1, 6, d), lambda b, i: (b, 0, 0)),
                  pl.BlockSpec((d, 2 * d_ff), lambda b, i: (0, 0), pipeline_mode=pl.Buffered(1)),
                  pl.BlockSpec((taps, 2 * d_ff), lambda b, i: (0, 0)),
                  pl.BlockSpec((1, 2 * d_ff), lambda b, i: (0, 0)),
                  pl.BlockSpec((d_ff, d), lambda b, i: (0, 0), pipeline_mode=pl.Buffered(1)),
                  pl.BlockSpec((1, d), lambda b, i: (0, 0)),
                  pl.BlockSpec((1, d), lambda b, i: (0, 0))],
        out_specs=pl.BlockSpec((1, tm, d), lambda b, i: (b, i, 0)),
        out_shape=jax.ShapeDtypeStruct((bsz, s, d), F32),
        scratch_shapes=[pltpu.VMEM((tm, d), BF16),
                        pltpu.VMEM((HALO, 2 * d_ff), F32),
                        pltpu.VMEM((HALO + tm, chunk), F32),
                        pltpu.VMEM((tm, d), F32)],
        compiler_params=_cparams(("parallel", "arbitrary")),
        name="ffn_ln",
    )(x, mod3, w_up, conv_w, conv_b.reshape(1, 2 * d_ff), w_down, g.reshape(1, d), b.reshape(1, d))


def _pad_cols(a, width):
    return jnp.pad(a, ((0, 0), (0, width - a.shape[1])))


def kernel(x, c, ada_w, ada_b, mix_in_w, ssm_conv_w, ssm_conv_b, ssm_dt_bias, ssm_a_log, ssm_d, ssm_norm_w,
           mix_out_w, ln1_g, ln1_b, ffn_up_w, ffn_conv_w, ffn_conv_b, ffn_down_w, ln2_g, ln2_b):
    depth = ada_w.shape[0]
    bsz, s, d = x.shape
    assert s % max(MOBA_BLOCK, SSD_CHUNK) == 0
    d_attn = N_ATTN_HEADS * HEAD_DIM
    d_ssm = N_SSM_HEADS * SSM_HEAD_DIM
    d_xbc = d_ssm + 2 * N_SSM_GROUPS * D_STATE
    hpg = N_SSM_HEADS // N_SSM_GROUPS
    alpha = (2.0 * depth) ** 0.25

    slope_vals = 2.0 ** (-8.0 * jnp.arange(1, N_ATTN_HEADS + 1, dtype=F32) / N_ATTN_HEADS)
    slopes = jnp.broadcast_to(slope_vals.reshape(N_ATTN_HEADS // 2, 2, 1), (N_ATTN_HEADS // 2, 2, MOBA_BLOCK))

    def per_group_lanes(v):
        return _pad_cols(v.reshape(N_SSM_GROUPS, hpg), LANES).reshape(1, N_SSM_GROUPS * LANES)

    for li in range(depth):
        mod3 = _adaln(c, ada_w[li], ada_b[li]).reshape(bsz, 6, d)

        w_in = mix_in_w[li]
        n_main = 3 * d_attn + d_ssm + d_xbc
        w_dt = w_in[:, n_main:].reshape(d, N_SSM_GROUPS, hpg)
        w_dt = jnp.pad(w_dt, ((0, 0), (0, 0), (0, LANES - hpg))).reshape(d, N_SSM_GROUPS * LANES)
        w_pad = jnp.concatenate([w_in[:, :n_main], w_dt], axis=1).astype(BF16)
        segs = ((0, 3 * d_attn), (3 * d_attn, d_ssm), (3 * d_attn + d_ssm, d_xbc), (n_main, N_SSM_GROUPS * LANES))
        qkv, z, xbc, dt = _inproj(x, mod3, w_pad, segs)

        o_attn = _attention(qkv, slopes)
        y_ssm = _ssd(xbc, dt, z, ssm_conv_w[li], ssm_conv_b[li].reshape(1, d_xbc),
                     per_group_lanes(ssm_dt_bias[li]), per_group_lanes(ssm_a_log[li]),
                     jnp.repeat(ssm_d[li], SSM_HEAD_DIM).reshape(1, d_ssm), ssm_norm_w[li].reshape(1, d_ssm))

        x = _outproj(o_attn, y_ssm, x, mod3, mix_out_w[li].astype(BF16), ln1_g[li], ln1_b[li], alpha)
        x = _ffn(x, mod3, ffn_up_w[li].astype(BF16), ffn_conv_w[li], ffn_conv_b[li],
                 ffn_down_w[li].astype(BF16), ln2_g[li], ln2_b[li], alpha)
    return x
```

```python
import functools

import jax
import jax.numpy as jnp
from jax import lax
from jax.experimental import pallas as pl
from jax.experimental.pallas import tpu as pltpu

F32 = jnp.float32
BF16 = jnp.bfloat16

N_ATTN_HEADS = 8
HEAD_DIM = 64
MOBA_BLOCK = 256
MOBA_TOPK = 3
N_SSM_HEADS = 16
SSM_HEAD_DIM = 64
N_SSM_GROUPS = 2
D_STATE = 128
SSD_CHUNK = 256
LN_EPS = 1e-5
RMS_EPS = 1e-5

LANES = 128
HALO = 8
NEG_BIG = -1e30
VMEM_LIMIT = 52 * 1024 * 1024


def _cparams(sem):
    return pltpu.CompilerParams(dimension_semantics=sem, vmem_limit_bytes=VMEM_LIMIT)


def _sigmoid(x):
    return 1.0 / (1.0 + jnp.exp(-x))


def _dot(a, b):
    return jnp.dot(a, b, preferred_element_type=F32)


def _dot_nt(a, b):
    return lax.dot_general(a, b, (((1,), (1,)), ((), ())), preferred_element_type=F32)


def _layer_norm(r, g, b):
    mu = jnp.mean(r, axis=-1, keepdims=True)
    d = r - mu
    var = jnp.mean(d * d, axis=-1, keepdims=True)
    return d * lax.rsqrt(var + LN_EPS) * g + b


def _adaln_kernel(c_ref, w_ref, b_ref, o_ref):
    c = c_ref[...]
    ca = (c * _sigmoid(c)).astype(BF16)
    o_ref[...] = _dot(ca, w_ref[...].astype(BF16)) + b_ref[...]


def _adaln(c, w, b):
    bsz, d = c.shape
    n = w.shape[1]
    tn = d
    return pl.pallas_call(
        _adaln_kernel,
        grid=(n // tn,),
        in_specs=[pl.BlockSpec((bsz, d), lambda j: (0, 0)),
                  pl.BlockSpec((d, tn), lambda j: (0, j)),
                  pl.BlockSpec((1, tn), lambda j: (0, j))],
        out_specs=pl.BlockSpec((bsz, tn), lambda j: (0, j)),
        out_shape=jax.ShapeDtypeStruct((bsz, n), F32),
        compiler_params=_cparams(("parallel",)),
        name="adaln",
    )(c, w, b.reshape(1, n))


def _inproj_kernel(x_ref, mod_ref, w_ref, qkv_ref, z_ref, xbc_ref, dt_ref, h_scr, *, segs, chunk):
    sh = mod_ref[0, 0:1, :]
    sc = mod_ref[0, 1:2, :]
    h_scr[...] = (x_ref[0] * (1.0 + sc) + sh).astype(BF16)
    outs = (qkv_ref, z_ref, xbc_ref, dt_ref)
    for dst, (c0, width) in zip(outs, segs):
        for a in range(0, width, chunk):
            n = min(chunk, width - a)
            dst[0, :, a:a + n] = _dot(h_scr[...], w_ref[:, c0 + a:c0 + a + n]).astype(dst.dtype)


def _inproj(x, mod3, w_pad, segs, tm=512, chunk=512):
    bsz, s, d = x.shape
    ntot = w_pad.shape[1]
    widths = [w for _, w in segs]
    dts = [BF16, F32, F32, F32]
    kern = functools.partial(_inproj_kernel, segs=segs, chunk=chunk)
    return pl.pallas_call(
        kern,
        grid=(bsz, s // tm),
        in_specs=[pl.BlockSpec((1, tm, d), lambda b, i: (b, i, 0)),
                  pl.BlockSpec((1, 6, d), lambda b, i: (b, 0, 0)),
                  pl.BlockSpec((d, ntot), lambda b, i: (0, 0), pipeline_mode=pl.Buffered(1))],
        out_specs=[pl.BlockSpec((1, tm, w), lambda b, i: (b, i, 0)) for w in widths],
        out_shape=[jax.ShapeDtypeStruct((bsz, s, w), dt) for w, dt in zip(widths, dts)],
        scratch_shapes=[pltpu.VMEM((tm, d), BF16)],
        compiler_params=_cparams(("parallel", "parallel")),
        name="inproj",
    )(x, mod3, w_pad)


def _attn_kernel(q_ref, k_ref, v_ref, tab_ref, slope_ref, o_ref,
                 kt_scr, va_scr, kmean_scr, *, nb, scale):
    blk = MOBA_BLOCK
    i = pl.program_id(2)
    lane = lax.broadcasted_iota(jnp.int32, (blk, LANES), 1)

    @pl.when(i == 0)
    def _():
        kmean_scr[...] = jnp.zeros_like(kmean_scr)
        for j in range(nb):
            kj = k_ref[0, j * blk:(j + 1) * blk, :]
            vj = v_ref[0, j * blk:(j + 1) * blk, :]
            kmean_scr[j:j + 1, :] = jnp.mean(kj.astype(F32), axis=0, keepdims=True)
            for hh in range(2):
                own = (lane // HEAD_DIM) == hh
                k_aug = jnp.where(own, kj, tab_ref[hh, j * blk:(j + 1) * blk, :])
                kt_scr[hh, :, j * blk:(j + 1) * blk] = k_aug.astype(F32).T.astype(BF16)
                va_scr[hh, j * blk:(j + 1) * blk, :] = jnp.where(own, vj, jnp.ones_like(vj))

    q = q_ref[0]
    q_scaled = q * jnp.asarray(scale, BF16)
    kmean = kmean_scr[...].astype(BF16)
    row = lax.broadcasted_iota(jnp.int32, (blk, blk), 0)
    col = lax.broadcasted_iota(jnp.int32, (blk, blk), 1)
    causal = row >= col
    blkid = lax.broadcasted_iota(jnp.int32, (8, blk), 0)
    past = blkid < i
    ones_t = jnp.where(blkid < 2, 1.0, 0.0).astype(F32)

    q_augs = []
    for hh in range(2):
        own = (lane // HEAD_DIM) == hh
        qh = jnp.where(own, q, jnp.zeros_like(q))

        gate = _dot_nt(kmean, qh)[0:8, :]
        gate = jnp.where(past, gate, -jnp.inf)
        rank = jnp.zeros((8, blk), F32)
        for jp in range(nb):
            gj = gate[jp:jp + 1, :]
            beats = (gj > gate) | ((gj == gate) & (blkid > jp))
            rank = rank + jnp.where(beats, 1.0, 0.0)
        sel = (rank < MOBA_TOPK) & past
        slope = slope_ref[0, hh:hh + 1, :]
        dist = ((i - blkid) * blk).astype(F32)
        bias_t = jnp.where(sel, -slope * dist, NEG_BIG)
        bias_t = jnp.where(blkid == i, 0.0, bias_t)
        pieces = [bias_t, ones_t, jnp.zeros((HEAD_DIM - 16, blk), F32)]
        zeros_own = jnp.zeros((HEAD_DIM, blk), F32)
        pieces = [zeros_own] + pieces if hh == 0 else pieces + [zeros_own]
        extra = jnp.concatenate(pieces, axis=0).T
        q_augs.append(jnp.where(own, q_scaled, extra.astype(BF16)))

    def attend(n):
        outs = []
        for hh in range(2):
            s = _dot(q_augs[hh], kt_scr[hh, :, 0:n * blk])
            parts = [s[:, j * blk:(j + 1) * blk] for j in range(n)]
            parts[-1] = jnp.where(causal, parts[-1], NEG_BIG)
            m = parts[-1]
            for part in parts[:-1]:
                m = jnp.maximum(m, part)
            m = jnp.max(m, axis=-1, keepdims=True)
            p = jnp.concatenate([jnp.exp(part - m).astype(BF16) for part in parts], axis=1)
            acc = _dot(p, va_scr[hh, 0:n * blk, :])
            outs.append(acc / pltpu.roll(acc, HEAD_DIM, axis=1))
        o_ref[0] = jnp.where((lane // HEAD_DIM) == 0, outs[0], outs[1]).astype(o_ref.dtype)

    for n in range(1, nb + 1):
        pl.when(i == n - 1)(functools.partial(attend, n))


def _attention(qkv, tab, slopes):
    bsz, s, _ = qkv.shape
    blk = MOBA_BLOCK
    nb = s // blk
    assert nb <= 8
    npairs = N_ATTN_HEADS // 2
    kern = functools.partial(_attn_kernel, nb=nb, scale=HEAD_DIM ** -0.5)
    return pl.pallas_call(
        kern,
        grid=(bsz, npairs, nb),
        in_specs=[pl.BlockSpec((1, blk, LANES), lambda b, hp, i: (b, i, hp)),
                  pl.BlockSpec((1, s, LANES), lambda b, hp, i: (b, 0, npairs + hp)),
                  pl.BlockSpec((1, s, LANES), lambda b, hp, i: (b, 0, 2 * npairs + hp)),
                  pl.BlockSpec((2, s, LANES), lambda b, hp, i: (hp, 0, 0)),
                  pl.BlockSpec((1, 2, blk), lambda b, hp, i: (hp, 0, 0))],
        out_specs=pl.BlockSpec((1, blk, LANES), lambda b, hp, i: (b, i, hp)),
        out_shape=jax.ShapeDtypeStruct((bsz, s, npairs * LANES), BF16),
        scratch_shapes=[pltpu.VMEM((2, LANES, s), BF16),
                        pltpu.VMEM((2, s, LANES), BF16),
                        pltpu.VMEM((16, LANES), F32)],
        compiler_params=_cparams(("parallel", "parallel", "arbitrary")),
        name="moba_attn",
    )(qkv, qkv, qkv, tab, slopes)


def _attn_key_table(slope_vals, s):
    pos = jnp.arange(s)
    blk_of = pos // MOBA_BLOCK
    within = (pos % MOBA_BLOCK).astype(F32)
    tabs = []
    for h in range(N_ATTN_HEADS):
        ind = (blk_of[:, None] == jnp.arange(8)[None, :]).astype(F32)
        val = slope_vals[h] * within
        hi = val.astype(BF16).astype(F32)
        lo = val - hi
        spare = jnp.concatenate([ind, hi[:, None], lo[:, None], jnp.zeros((s, HEAD_DIM - 10), F32)], axis=1)
        own = jnp.zeros((s, HEAD_DIM), F32)
        tabs.append(jnp.concatenate([own, spare] if h % 2 == 0 else [spare, own], axis=1))
    return jnp.stack(tabs).astype(BF16)


def _conv_silu(x_ref, halo_ref, ext_ref, w_ref, b_ref, first):
    rows = x_ref.shape[1]
    taps = w_ref.shape[0]

    @pl.when(first)
    def _():
        halo_ref[...] = jnp.zeros_like(halo_ref)

    ext_ref[0:HALO, :] = halo_ref[...]
    ext_ref[HALO:HALO + rows, :] = x_ref[0]
    halo_ref[...] = x_ref[0, rows - HALO:rows, :]
    acc = b_ref[...] + w_ref[taps - 1:taps, :] * ext_ref[HALO:HALO + rows, :]
    for j in range(taps - 1):
        off = HALO - (taps - 1) + j
        acc = acc + w_ref[j:j + 1, :] * ext_ref[off:off + rows, :]
    return acc * _sigmoid(acc)


def _ssd_kernel(xs_ref, bm_ref, cm_ref, dt_ref, z_ref,
                wx_ref, wb_ref, wc_ref, bx_ref, bb_ref, bc_ref,
                dtb_ref, alog_ref, dskip_ref, nw_ref,
                y_ref,
                hx_scr, hb_scr, hc_scr, ex_scr, eb_scr, ec_scr, state_scr, xdec_scr, ydiag_scr):
    L = SSD_CHUNK
    hpg = N_SSM_HEADS // N_SSM_GROUPS
    first = pl.program_id(2) == 0

    xs = _conv_silu(xs_ref, hx_scr, ex_scr, wx_ref, bx_ref, first)
    bm = _conv_silu(bm_ref, hb_scr, eb_scr, wb_ref, bb_ref, first)
    cm = _conv_silu(cm_ref, hc_scr, ec_scr, wc_ref, bc_ref, first)

    @pl.when(first)
    def _():
        state_scr[...] = jnp.zeros_like(state_scr)

    xdt_in = dt_ref[0] + dtb_ref[...]
    dt = jnp.maximum(xdt_in, 0.0) + jnp.log1p(jnp.exp(-jnp.abs(xdt_in)))
    a = -jnp.exp(alog_ref[...])
    da = dt * a
    row = lax.broadcasted_iota(jnp.int32, (L, L), 0)
    col = lax.broadcasted_iota(jnp.int32, (L, L), 1)
    causal = row >= col
    tri = jnp.where(causal, 1.0, 0.0).astype(F32)
    tri_t = jnp.where(row <= col, 1.0, 0.0).astype(F32)
    cs = jnp.dot(tri, da, preferred_element_type=F32, precision=lax.Precision.HIGHEST)
    cs_t = jnp.dot(da.T, tri_t, preferred_element_type=F32, precision=lax.Precision.HIGHEST)
    cs_last = cs[L - 1:L, :]

    bm16 = bm.astype(BF16)
    cm16 = cm.astype(BF16)
    cb = _dot_nt(cm16, bm16)
    lane = lax.broadcasted_iota(jnp.int32, (L, LANES), 1)
    lo_half = lane < SSM_HEAD_DIM

    cd_parts, eo_parts, xdt_parts = [], [], []
    for pr in range(hpg // 2):
        h0, h1 = 2 * pr, 2 * pr + 1
        xs_p = xs[:, pr * LANES:(pr + 1) * LANES]
        cs_p = jnp.where(lo_half, cs[:, h0:h0 + 1], cs[:, h1:h1 + 1])
        dt_p = jnp.where(lo_half, dt[:, h0:h0 + 1], dt[:, h1:h1 + 1])
        csl_p = jnp.where(lo_half[0:1], cs_last[:, h0:h0 + 1], cs_last[:, h1:h1 + 1])
        xdt_p = xs_p * dt_p
        xdt16 = xdt_p.astype(BF16)
        xdec_scr[:, pr * LANES:(pr + 1) * LANES] = (xdt_p * jnp.exp(csl_p - cs_p)).astype(BF16)
        cd_parts.append(jnp.exp(csl_p))
        eo_parts.append(jnp.exp(cs_p))
        ys = []
        for h in (h0, h1):
            diff = cs[:, h:h + 1] - cs_t[h:h + 1, :]
            lmat = jnp.exp(jnp.where(causal, diff, -jnp.inf))
            ys.append(_dot((cb * lmat).astype(BF16), xdt16))
        ydiag_scr[:, pr * LANES:(pr + 1) * LANES] = jnp.where(lo_half, ys[0], ys[1])

    eo = jnp.concatenate(eo_parts, axis=-1)
    cd = jnp.concatenate(cd_parts, axis=-1)
    state = state_scr[...]
    y = ydiag_scr[...] + eo * _dot(cm16, state.astype(BF16)) + xs * dskip_ref[...]
    state_scr[...] = state * cd + _dot(bm.T.astype(BF16), xdec_scr[...])

    z = z_ref[0]
    y = y * (z * _sigmoid(z))
    y = y * lax.rsqrt(jnp.mean(y * y, axis=-1, keepdims=True) + RMS_EPS) * nw_ref[...]
    y_ref[0] = y.astype(y_ref.dtype)


def _ssd(xbc, dt, z, conv_w, conv_b, dtb, alog, dskip, norm_w):
    bsz, s, _ = xbc.shape
    L = SSD_CHUNK
    nc = s // L
    G = N_SSM_GROUPS
    gw = (N_SSM_HEADS // G) * SSM_HEAD_DIM
    d_ssm = N_SSM_HEADS * SSM_HEAD_DIM
    taps = conv_w.shape[0]
    xo = gw // LANES
    bo = d_ssm // LANES
    co = bo + G * D_STATE // LANES

    def seq(width, off_fn):
        return pl.BlockSpec((1, L, width), lambda b, g, c: (b, c, off_fn(g)))

    def par(rows, width, off_fn):
        return pl.BlockSpec((rows, width), lambda b, g, c: (0, off_fn(g)))

    in_specs = [
        seq(gw, lambda g: g), seq(D_STATE, lambda g: bo + g), seq(D_STATE, lambda g: co + g),
        seq(LANES, lambda g: g), seq(gw, lambda g: g),
        par(taps, gw, lambda g: g), par(taps, D_STATE, lambda g: bo + g), par(taps, D_STATE, lambda g: co + g),
        par(1, gw, lambda g: g), par(1, D_STATE, lambda g: bo + g), par(1, D_STATE, lambda g: co + g),
        par(1, LANES, lambda g: g), par(1, LANES, lambda g: g),
        par(1, gw, lambda g: g), par(1, gw, lambda g: g),
    ]
    return pl.pallas_call(
        _ssd_kernel,
        grid=(bsz, G, nc),
        in_specs=in_specs,
        out_specs=pl.BlockSpec((1, L, gw), lambda b, g, c: (b, c, g)),
        out_shape=jax.ShapeDtypeStruct((bsz, s, d_ssm), BF16),
        scratch_shapes=[pltpu.VMEM((HALO, gw), F32), pltpu.VMEM((HALO, D_STATE), F32), pltpu.VMEM((HALO, D_STATE), F32),
                        pltpu.VMEM((HALO + L, gw), F32), pltpu.VMEM((HALO + L, D_STATE), F32),
                        pltpu.VMEM((HALO + L, D_STATE), F32),
                        pltpu.VMEM((D_STATE, gw), F32), pltpu.VMEM((L, gw), BF16), pltpu.VMEM((L, gw), F32)],
        compiler_params=_cparams(("parallel", "parallel", "arbitrary")),
        name="ssd",
    )(xbc, xbc, xbc, dt, z, conv_w, conv_w, conv_w, conv_b, conv_b, conv_b, dtb, alog, dskip, norm_w)


def _outproj_kernel(oa_ref, y_ref, x_ref, mod_ref, w_ref, g_ref, b_ref, o_ref, *, alpha, d_attn):
    acc = _dot(oa_ref[0], w_ref[0:d_attn, :]) + _dot(y_ref[0], w_ref[d_attn:, :])
    gate = mod_ref[0, 2:3, :]
    r = alpha * x_ref[0] + (1.0 + gate) * acc
    o_ref[0] = _layer_norm(r, g_ref[...], b_ref[...])


def _outproj(oa, y, x, mod3, w, g, b, alpha, tm=512):
    bsz, s, d = x.shape
    d_attn = oa.shape[-1]
    d_ssm = y.shape[-1]
    kern = functools.partial(_outproj_kernel, alpha=alpha, d_attn=d_attn)
    return pl.pallas_call(
        kern,
        grid=(bsz, s // tm),
        in_specs=[pl.BlockSpec((1, tm, d_attn), lambda b, i: (b, i, 0)),
                  pl.BlockSpec((1, tm, d_ssm), lambda b, i: (b, i, 0)),
                  pl.BlockSpec((1, tm, d), lambda b, i: (b, i, 0)),
                  pl.BlockSpec((1, 6, d), lambda b, i: (b, 0, 0)),
                  pl.BlockSpec((d_attn + d_ssm, d), lambda b, i: (0, 0), pipeline_mode=pl.Buffered(1)),
                  pl.BlockSpec((1, d), lambda b, i: (0, 0)),
                  pl.BlockSpec((1, d), lambda b, i: (0, 0))],
        out_specs=pl.BlockSpec((1, tm, d), lambda b, i: (b, i, 0)),
        out_shape=jax.ShapeDtypeStruct((bsz, s, d), F32),
        compiler_params=_cparams(("parallel", "parallel")),
        name="outproj_ln",
    )(oa, y, x, mod3, w, g.reshape(1, d), b.reshape(1, d))


def _ffn_kernel(x_ref, mod_ref, wu_ref, cw_ref, cb_ref, wd_ref, g_ref, b_ref, o_ref,
                h_scr, halo_scr, ext_scr, acc_scr, *, alpha, d_ff, chunk):
    tm = x_ref.shape[1]
    taps = cw_ref.shape[0]
    first = pl.program_id(1) == 0

    @pl.when(first)
    def _():
        halo_scr[...] = jnp.zeros_like(halo_scr)

    x = x_ref[0]
    sh = mod_ref[0, 3:4, :]
    sc = mod_ref[0, 4:5, :]
    h_scr[...] = (x * (1.0 + sc) + sh).astype(BF16)

    def conv_cols(c0):
        u = _dot(h_scr[...], wu_ref[:, c0:c0 + chunk])
        ext_scr[0:HALO, :] = halo_scr[:, c0:c0 + chunk]
        ext_scr[HALO:HALO + tm, :] = u
        halo_scr[:, c0:c0 + chunk] = u[tm - HALO:tm, :]
        out = cb_ref[:, c0:c0 + chunk] + cw_ref[taps - 1:taps, c0:c0 + chunk] * u
        for j in range(taps - 1):
            off = HALO - (taps - 1) + j
            out = out + cw_ref[j:j + 1, c0:c0 + chunk] * ext_scr[off:off + tm, :]
        return out

    for ci, c0 in enumerate(range(0, d_ff, chunk)):
        gte = conv_cols(c0)
        val = conv_cols(d_ff + c0)
        act = (gte * _sigmoid(gte) * val).astype(BF16)
        part = _dot(act, wd_ref[c0:c0 + chunk, :])
        if ci == 0:
            acc_scr[...] = part
        else:
            acc_scr[...] += part

    gate = mod_ref[0, 5:6, :]
    r = alpha * x + (1.0 + gate) * acc_scr[...]
    o_ref[0] = _layer_norm(r, g_ref[...], b_ref[...])


def _ffn(x, mod3, w_up, conv_w, conv_b, w_down, g, b, alpha, tm=256, chunk=256):
    bsz, s, d = x.shape
    d_ff = w_down.shape[0]
    taps = conv_w.shape[0]
    kern = functools.partial(_ffn_kernel, alpha=alpha, d_ff=d_ff, chunk=chunk)
    return pl.pallas_call(
        kern,
        grid=(bsz, s // tm),
        in_specs=[pl.BlockSpec((1, tm, d), lambda b, i: (b, i, 0)),
                  pl.BlockSpec((1, 6, d), lambda b, i: (b, 0, 0)),
                  pl.BlockSpec((d, 2 * d_ff), lambda b, i: (0, 0), pipeline_mode=pl.Buffered(1)),
                  pl.BlockSpec((taps, 2 * d_ff), lambda b, i: (0, 0)),
                  pl.BlockSpec((1, 2 * d_ff), lambda b, i: (0, 0)),
                  pl.BlockSpec((d_ff, d), lambda b, i: (0, 0), pipeline_mode=pl.Buffered(1)),
                  pl.BlockSpec((1, d), lambda b, i: (0, 0)),
                  pl.BlockSpec((1, d), lambda b, i: (0, 0))],
        out_specs=pl.BlockSpec((1, tm, d), lambda b, i: (b, i, 0)),
        out_shape=jax.ShapeDtypeStruct((bsz, s, d), F32),
        scratch_shapes=[pltpu.VMEM((tm, d), BF16),
                        pltpu.VMEM((HALO, 2 * d_ff), F32),
                        pltpu.VMEM((HALO + tm, chunk), F32),
                        pltpu.VMEM((tm, d), F32)],
        compiler_params=_cparams(("parallel", "arbitrary")),
        name="ffn_ln",
    )(x, mod3, w_up, conv_w, conv_b.reshape(1, 2 * d_ff), w_down, g.reshape(1, d), b.reshape(1, d))


def _pad_cols(a, width):
    return jnp.pad(a, ((0, 0), (0, width - a.shape[1])))


def kernel(x, c, ada_w, ada_b, mix_in_w, ssm_conv_w, ssm_conv_b, ssm_dt_bias, ssm_a_log, ssm_d, ssm_norm_w,
           mix_out_w, ln1_g, ln1_b, ffn_up_w, ffn_conv_w, ffn_conv_b, ffn_down_w, ln2_g, ln2_b):
    depth = ada_w.shape[0]
    bsz, s, d = x.shape
    assert s % max(MOBA_BLOCK, SSD_CHUNK) == 0
    d_attn = N_ATTN_HEADS * HEAD_DIM
    d_ssm = N_SSM_HEADS * SSM_HEAD_DIM
    d_xbc = d_ssm + 2 * N_SSM_GROUPS * D_STATE
    hpg = N_SSM_HEADS // N_SSM_GROUPS
    alpha = (2.0 * depth) ** 0.25

    slope_vals = 2.0 ** (-8.0 * jnp.arange(1, N_ATTN_HEADS + 1, dtype=F32) / N_ATTN_HEADS)
    slopes = jnp.broadcast_to(slope_vals.reshape(N_ATTN_HEADS // 2, 2, 1), (N_ATTN_HEADS // 2, 2, MOBA_BLOCK))
    key_tab = _attn_key_table(slope_vals, s)

    def per_group_lanes(v):
        return _pad_cols(v.reshape(N_SSM_GROUPS, hpg), LANES).reshape(1, N_SSM_GROUPS * LANES)

    for li in range(depth):
        mod3 = _adaln(c, ada_w[li], ada_b[li]).reshape(bsz, 6, d)

        w_in = mix_in_w[li]
        n_main = 3 * d_attn + d_ssm + d_xbc
        w_dt = w_in[:, n_main:].reshape(d, N_SSM_GROUPS, hpg)
        w_dt = jnp.pad(w_dt, ((0, 0), (0, 0), (0, LANES - hpg))).reshape(d, N_SSM_GROUPS * LANES)
        w_pad = jnp.concatenate([w_in[:, :n_main], w_dt], axis=1).astype(BF16)
        segs = ((0, 3 * d_attn), (3 * d_attn, d_ssm), (3 * d_attn + d_ssm, d_xbc), (n_main, N_SSM_GROUPS * LANES))
        qkv, z, xbc, dt = _inproj(x, mod3, w_pad, segs)

        o_attn = _attention(qkv, key_tab, slopes)
        y_ssm = _ssd(xbc, dt, z, ssm_conv_w[li], ssm_conv_b[li].reshape(1, d_xbc),
                     per_group_lanes(ssm_dt_bias[li]), per_group_lanes(ssm_a_log[li]),
                     jnp.repeat(ssm_d[li], SSM_HEAD_DIM).reshape(1, d_ssm), ssm_norm_w[li].reshape(1, d_ssm))

        x = _outproj(o_attn, y_ssm, x, mod3, mix_out_w[li].astype(BF16), ln1_g[li], ln1_b[li], alpha)
        x = _ffn(x, mod3, ffn_up_w[li].astype(BF16), ffn_conv_w[li], ffn_conv_b[li],
                 ffn_down_w[li].astype(BF16), ln2_g[li], ln2_b[li], alpha)
    return x
```

```python
import functools

import jax
import jax.numpy as jnp
from jax import lax
from jax.experimental import pallas as pl
from jax.experimental.pallas import tpu as pltpu

F32 = jnp.float32
BF16 = jnp.bfloat16

N_ATTN_HEADS = 8
HEAD_DIM = 64
MOBA_BLOCK = 256
MOBA_TOPK = 3
N_SSM_HEADS = 16
SSM_HEAD_DIM = 64
N_SSM_GROUPS = 2
D_STATE = 128
SSD_CHUNK = 256
LN_EPS = 1e-5
RMS_EPS = 1e-5

LANES = 128
HALO = 8
NEG_BIG = -1e30
VMEM_LIMIT = 52 * 1024 * 1024


def _cparams(sem):
    return pltpu.CompilerParams(dimension_semantics=sem, vmem_limit_bytes=VMEM_LIMIT)


def _sigmoid(x):
    return 1.0 / (1.0 + jnp.exp(-x))


def _dot(a, b):
    return jnp.dot(a, b, preferred_element_type=F32)


def _dot_nt(a, b):
    return lax.dot_general(a, b, (((1,), (1,)), ((), ())), preferred_element_type=F32)


def _causal_conv_rows(ext, w):
    taps = w.shape[0]
    acc = w[0:1, :] * ext
    for j in range(1, taps):
        acc = pltpu.roll(acc, 1, axis=0) + w[j:j + 1, :] * ext
    return acc[HALO:, :]


def _layer_norm(r, g, b):
    mu = jnp.mean(r, axis=-1, keepdims=True)
    d = r - mu
    var = jnp.mean(d * d, axis=-1, keepdims=True)
    return d * lax.rsqrt(var + LN_EPS) * g + b


def _adaln_kernel(c_ref, w_ref, b_ref, o_ref):
    c = c_ref[...]
    ca = (c * _sigmoid(c)).astype(BF16)
    o_ref[...] = _dot(ca, w_ref[...].astype(BF16)) + b_ref[...]


def _adaln(c, w, b):
    bsz, d = c.shape
    n = w.shape[1]
    tn = d
    return pl.pallas_call(
        _adaln_kernel,
        grid=(n // tn,),
        in_specs=[pl.BlockSpec((bsz, d), lambda j: (0, 0)),
                  pl.BlockSpec((d, tn), lambda j: (0, j)),
                  pl.BlockSpec((1, tn), lambda j: (0, j))],
        out_specs=pl.BlockSpec((bsz, tn), lambda j: (0, j)),
        out_shape=jax.ShapeDtypeStruct((bsz, n), F32),
        compiler_params=_cparams(("parallel",)),
        name="adaln",
    )(c, w, b.reshape(1, n))


def _inproj_kernel(x_ref, mod_ref, w_ref, qkv_ref, z_ref, xbc_ref, dt_ref, h_scr, *, segs, chunk):
    sh = mod_ref[0, 0:1, :]
    sc = mod_ref[0, 1:2, :]
    h_scr[...] = (x_ref[0] * (1.0 + sc) + sh).astype(BF16)
    outs = (qkv_ref, z_ref, xbc_ref, dt_ref)
    for dst, (c0, width) in zip(outs, segs):
        for a in range(0, width, chunk):
            n = min(chunk, width - a)
            dst[0, :, a:a + n] = _dot(h_scr[...], w_ref[:, c0 + a:c0 + a + n]).astype(dst.dtype)


def _inproj(x, mod3, w_pad, segs, tm=512, chunk=512):
    bsz, s, d = x.shape
    ntot = w_pad.shape[1]
    widths = [w for _, w in segs]
    dts = [BF16, F32, F32, F32]
    kern = functools.partial(_inproj_kernel, segs=segs, chunk=chunk)
    return pl.pallas_call(
        kern,
        grid=(bsz, s // tm),
        in_specs=[pl.BlockSpec((1, tm, d), lambda b, i: (b, i, 0)),
                  pl.BlockSpec((1, 6, d), lambda b, i: (b, 0, 0)),
                  pl.BlockSpec((d, ntot), lambda b, i: (0, 0), pipeline_mode=pl.Buffered(1))],
        out_specs=[pl.BlockSpec((1, tm, w), lambda b, i: (b, i, 0)) for w in widths],
        out_shape=[jax.ShapeDtypeStruct((bsz, s, w), dt) for w, dt in zip(widths, dts)],
        scratch_shapes=[pltpu.VMEM((tm, d), BF16)],
        compiler_params=_cparams(("parallel", "parallel")),
        name="inproj",
    )(x, mod3, w_pad)


def _attn_kernel(q_ref, k_ref, v_ref, tab_ref, slope_ref, o_ref,
                 kt_scr, va_scr, kmean_scr, *, nb, scale):
    blk = MOBA_BLOCK
    i = pl.program_id(2)
    lane = lax.broadcasted_iota(jnp.int32, (blk, LANES), 1)

    @pl.when(i == 0)
    def _():
        kmean_scr[...] = jnp.zeros_like(kmean_scr)
        for j in range(nb):
            kj = k_ref[0, j * blk:(j + 1) * blk, :]
            vj = v_ref[0, j * blk:(j + 1) * blk, :]
            kmean_scr[j:j + 1, :] = jnp.mean(kj.astype(F32), axis=0, keepdims=True)
            for hh in range(2):
                own = (lane // HEAD_DIM) == hh
                k_aug = jnp.where(own, kj, tab_ref[hh, j * blk:(j + 1) * blk, :])
                kt_scr[hh, :, j * blk:(j + 1) * blk] = k_aug.astype(F32).T.astype(BF16)
                va_scr[hh, j * blk:(j + 1) * blk, :] = jnp.where(own, vj, jnp.ones_like(vj))

    q = q_ref[0]
    q_scaled = q * jnp.asarray(scale, BF16)
    kmean = kmean_scr[...].astype(BF16)
    row = lax.broadcasted_iota(jnp.int32, (blk, blk), 0)
    col = lax.broadcasted_iota(jnp.int32, (blk, blk), 1)
    causal = row >= col
    blkid = lax.broadcasted_iota(jnp.int32, (8, blk), 0)
    past = blkid < i
    ones_t = jnp.where(blkid < 2, 1.0, 0.0).astype(F32)

    q_augs = []
    for hh in range(2):
        own = (lane // HEAD_DIM) == hh
        qh = jnp.where(own, q, jnp.zeros_like(q))

        gate = _dot_nt(kmean, qh)[0:8, :]
        gate = jnp.where(past, gate, -jnp.inf)
        rank = jnp.zeros((8, blk), F32)
        for jp in range(nb):
            gj = gate[jp:jp + 1, :]
            beats = (gj > gate) | ((gj == gate) & (blkid > jp))
            rank = rank + jnp.where(beats, 1.0, 0.0)
        sel = (rank < MOBA_TOPK) & past
        slope = slope_ref[0, hh:hh + 1, :]
        dist = ((i - blkid) * blk).astype(F32)
        bias_t = jnp.where(sel, -slope * dist, NEG_BIG)
        bias_t = jnp.where(blkid == i, 0.0, bias_t)
        pieces = [bias_t, ones_t, jnp.zeros((HEAD_DIM - 16, blk), F32)]
        zeros_own = jnp.zeros((HEAD_DIM, blk), F32)
        pieces = [zeros_own] + pieces if hh == 0 else pieces + [zeros_own]
        extra = jnp.concatenate(pieces, axis=0).T
        q_augs.append(jnp.where(own, q_scaled, extra.astype(BF16)))

    def attend(n):
        outs = []
        for hh in range(2):
            s = _dot(q_augs[hh], kt_scr[hh, :, 0:n * blk])
            parts = [s[:, j * blk:(j + 1) * blk] for j in range(n)]
            parts[-1] = jnp.where(causal, parts[-1], NEG_BIG)
            m = parts[-1]
            for part in parts[:-1]:
                m = jnp.maximum(m, part)
            m = jnp.max(m, axis=-1, keepdims=True)
            p = jnp.concatenate([jnp.exp(part - m).astype(BF16) for part in parts], axis=1)
            acc = _dot(p, va_scr[hh, 0:n * blk, :])
            outs.append(acc / pltpu.roll(acc, HEAD_DIM, axis=1))
        o_ref[0] = jnp.where((lane // HEAD_DIM) == 0, outs[0], outs[1]).astype(o_ref.dtype)

    for n in range(1, nb + 1):
        pl.when(i == n - 1)(functools.partial(attend, n))


def _attention(qkv, tab, slopes):
    bsz, s, _ = qkv.shape
    blk = MOBA_BLOCK
    nb = s // blk
    assert nb <= 8
    npairs = N_ATTN_HEADS // 2
    kern = functools.partial(_attn_kernel, nb=nb, scale=HEAD_DIM ** -0.5)
    return pl.pallas_call(
        kern,
        grid=(bsz, npairs, nb),
        in_specs=[pl.BlockSpec((1, blk, LANES), lambda b, hp, i: (b, i, hp)),
                  pl.BlockSpec((1, s, LANES), lambda b, hp, i: (b, 0, npairs + hp)),
                  pl.BlockSpec((1, s, LANES), lambda b, hp, i: (b, 0, 2 * npairs + hp)),
                  pl.BlockSpec((2, s, LANES), lambda b, hp, i: (hp, 0, 0)),
                  pl.BlockSpec((1, 2, blk), lambda b, hp, i: (hp, 0, 0))],
        out_specs=pl.BlockSpec((1, blk, LANES), lambda b, hp, i: (b, i, hp)),
        out_shape=jax.ShapeDtypeStruct((bsz, s, npairs * LANES), BF16),
        scratch_shapes=[pltpu.VMEM((2, LANES, s), BF16),
                        pltpu.VMEM((2, s, LANES), BF16),
                        pltpu.VMEM((16, LANES), F32)],
        compiler_params=_cparams(("parallel", "parallel", "arbitrary")),
        name="moba_attn",
    )(qkv, qkv, qkv, tab, slopes)


def _attn_key_table(slope_vals, s):
    pos = jnp.arange(s)
    blk_of = pos // MOBA_BLOCK
    within = (pos % MOBA_BLOCK).astype(F32)
    tabs = []
    for h in range(N_ATTN_HEADS):
        ind = (blk_of[:, None] == jnp.arange(8)[None, :]).astype(F32)
        val = slope_vals[h] * within
        hi = val.astype(BF16).astype(F32)
        lo = val - hi
        spare = jnp.concatenate([ind, hi[:, None], lo[:, None], jnp.zeros((s, HEAD_DIM - 10), F32)], axis=1)
        own = jnp.zeros((s, HEAD_DIM), F32)
        tabs.append(jnp.concatenate([own, spare] if h % 2 == 0 else [spare, own], axis=1))
    return jnp.stack(tabs).astype(BF16)


def _conv_silu(x_ref, halo_ref, w_ref, b_ref, first):
    rows = x_ref.shape[1]

    @pl.when(first)
    def _():
        halo_ref[...] = jnp.zeros_like(halo_ref)

    ext = jnp.concatenate([halo_ref[...], x_ref[0]], axis=0)
    halo_ref[...] = x_ref[0, rows - HALO:rows, :]
    acc = _causal_conv_rows(ext, w_ref[...]) + b_ref[...]
    return acc * _sigmoid(acc)


def _ssd_kernel(xs_ref, bm_ref, cm_ref, dt_ref, z_ref,
                wx_ref, wb_ref, wc_ref, bx_ref, bb_ref, bc_ref,
                dtb_ref, alog_ref, dskip_ref, nw_ref,
                y_ref,
                hx_scr, hb_scr, hc_scr, state_scr, xdec_scr, ydiag_scr):
    L = SSD_CHUNK
    hpg = N_SSM_HEADS // N_SSM_GROUPS
    first = pl.program_id(2) == 0

    xs = _conv_silu(xs_ref, hx_scr, wx_ref, bx_ref, first)
    bm = _conv_silu(bm_ref, hb_scr, wb_ref, bb_ref, first)
    cm = _conv_silu(cm_ref, hc_scr, wc_ref, bc_ref, first)

    @pl.when(first)
    def _():
        state_scr[...] = jnp.zeros_like(state_scr)

    xdt_in = dt_ref[0] + dtb_ref[...]
    dt = jnp.maximum(xdt_in, 0.0) + jnp.log1p(jnp.exp(-jnp.abs(xdt_in)))
    a = -jnp.exp(alog_ref[...])
    da = dt * a
    row = lax.broadcasted_iota(jnp.int32, (L, L), 0)
    col = lax.broadcasted_iota(jnp.int32, (L, L), 1)
    causal = row >= col
    tri = jnp.where(causal, 1.0, 0.0).astype(F32)
    tri_t = jnp.where(row <= col, 1.0, 0.0).astype(F32)
    cs = jnp.dot(tri, da, preferred_element_type=F32, precision=lax.Precision.HIGHEST)
    cs_t = jnp.dot(da.T, tri_t, preferred_element_type=F32, precision=lax.Precision.HIGHEST)
    cs_last = cs[L - 1:L, :]

    bm16 = bm.astype(BF16)
    cm16 = cm.astype(BF16)
    cb = _dot_nt(cm16, bm16)
    lane = lax.broadcasted_iota(jnp.int32, (L, LANES), 1)
    lo_half = lane < SSM_HEAD_DIM

    cd_parts, eo_parts, xdt_parts = [], [], []
    for pr in range(hpg // 2):
        h0, h1 = 2 * pr, 2 * pr + 1
        xs_p = xs[:, pr * LANES:(pr + 1) * LANES]
        cs_p = jnp.where(lo_half, cs[:, h0:h0 + 1], cs[:, h1:h1 + 1])
        dt_p = jnp.where(lo_half, dt[:, h0:h0 + 1], dt[:, h1:h1 + 1])
        csl_p = jnp.where(lo_half[0:1], cs_last[:, h0:h0 + 1], cs_last[:, h1:h1 + 1])
        xdt_p = xs_p * dt_p
        xdt16 = xdt_p.astype(BF16)
        xdec_scr[:, pr * LANES:(pr + 1) * LANES] = (xdt_p * jnp.exp(csl_p - cs_p)).astype(BF16)
        cd_parts.append(jnp.exp(csl_p))
        eo_parts.append(jnp.exp(cs_p))
        ys = []
        for h in (h0, h1):
            diff = cs[:, h:h + 1] - cs_t[h:h + 1, :]
            lmat = jnp.exp(jnp.where(causal, diff, -jnp.inf))
            ys.append(_dot((cb * lmat).astype(BF16), xdt16))
        ydiag_scr[:, pr * LANES:(pr + 1) * LANES] = jnp.where(lo_half, ys[0], ys[1])

    eo = jnp.concatenate(eo_parts, axis=-1)
    cd = jnp.concatenate(cd_parts, axis=-1)
    state = state_scr[...]
    y = ydiag_scr[...] + eo * _dot(cm16, state.astype(BF16)) + xs * dskip_ref[...]
    state_scr[...] = state * cd + _dot(bm.T.astype(BF16), xdec_scr[...])

    z = z_ref[0]
    y = y * (z * _sigmoid(z))
    y = y * lax.rsqrt(jnp.mean(y * y, axis=-1, keepdims=True) + RMS_EPS) * nw_ref[...]
    y_ref[0] = y.astype(y_ref.dtype)


def _ssd(xbc, dt, z, conv_w, conv_b, dtb, alog, dskip, norm_w):
    bsz, s, _ = xbc.shape
    L = SSD_CHUNK
    nc = s // L
    G = N_SSM_GROUPS
    gw = (N_SSM_HEADS // G) * SSM_HEAD_DIM
    d_ssm = N_SSM_HEADS * SSM_HEAD_DIM
    taps = conv_w.shape[0]
    xo = gw // LANES
    bo = d_ssm // LANES
    co = bo + G * D_STATE // LANES

    def seq(width, off_fn):
        return pl.BlockSpec((1, L, width), lambda b, g, c: (b, c, off_fn(g)))

    def par(rows, width, off_fn):
        return pl.BlockSpec((rows, width), lambda b, g, c: (0, off_fn(g)))

    in_specs = [
        seq(gw, lambda g: g), seq(D_STATE, lambda g: bo + g), seq(D_STATE, lambda g: co + g),
        seq(LANES, lambda g: g), seq(gw, lambda g: g),
        par(taps, gw, lambda g: g), par(taps, D_STATE, lambda g: bo + g), par(taps, D_STATE, lambda g: co + g),
        par(1, gw, lambda g: g), par(1, D_STATE, lambda g: bo + g), par(1, D_STATE, lambda g: co + g),
        par(1, LANES, lambda g: g), par(1, LANES, lambda g: g),
        par(1, gw, lambda g: g), par(1, gw, lambda g: g),
    ]
    return pl.pallas_call(
        _ssd_kernel,
        grid=(bsz, G, nc),
        in_specs=in_specs,
        out_specs=pl.BlockSpec((1, L, gw), lambda b, g, c: (b, c, g)),
        out_shape=jax.ShapeDtypeStruct((bsz, s, d_ssm), BF16),
        scratch_shapes=[pltpu.VMEM((HALO, gw), F32), pltpu.VMEM((HALO, D_STATE), F32), pltpu.VMEM((HALO, D_STATE), F32),
                        pltpu.VMEM((D_STATE, gw), F32), pltpu.VMEM((L, gw), BF16), pltpu.VMEM((L, gw), F32)],
        compiler_params=_cparams(("parallel", "parallel", "arbitrary")),
        name="ssd",
    )(xbc, xbc, xbc, dt, z, conv_w, conv_w, conv_w, conv_b, conv_b, conv_b, dtb, alog, dskip, norm_w)


def _outproj_kernel(oa_ref, y_ref, x_ref, mod_ref, w_ref, g_ref, b_ref, o_ref, *, alpha, d_attn):
    acc = _dot(oa_ref[0], w_ref[0:d_attn, :]) + _dot(y_ref[0], w_ref[d_attn:, :])
    gate = mod_ref[0, 2:3, :]
    r = alpha * x_ref[0] + (1.0 + gate) * acc
    o_ref[0] = _layer_norm(r, g_ref[...], b_ref[...])


def _outproj(oa, y, x, mod3, w, g, b, alpha, tm=512):
    bsz, s, d = x.shape
    d_attn = oa.shape[-1]
    d_ssm = y.shape[-1]
    kern = functools.partial(_outproj_kernel, alpha=alpha, d_attn=d_attn)
    return pl.pallas_call(
        kern,
        grid=(bsz, s // tm),
        in_specs=[pl.BlockSpec((1, tm, d_attn), lambda b, i: (b, i, 0)),
                  pl.BlockSpec((1, tm, d_ssm), lambda b, i: (b, i, 0)),
                  pl.BlockSpec((1, tm, d), lambda b, i: (b, i, 0)),
                  pl.BlockSpec((1, 6, d), lambda b, i: (b, 0, 0)),
                  pl.BlockSpec((d_attn + d_ssm, d), lambda b, i: (0, 0), pipeline_mode=pl.Buffered(1)),
                  pl.BlockSpec((1, d), lambda b, i: (0, 0)),
                  pl.BlockSpec((1, d), lambda b, i: (0, 0))],
        out_specs=pl.BlockSpec((1, tm, d), lambda b, i: (b, i, 0)),
        out_shape=jax.ShapeDtypeStruct((bsz, s, d), F32),
        compiler_params=_cparams(("parallel", "parallel")),
        name="outproj_ln",
    )(oa, y, x, mod3, w, g.reshape(1, d), b.reshape(1, d))


def _ffn_kernel(x_ref, mod_ref, wu_ref, cw_ref, cb_ref, wd_ref, g_ref, b_ref, o_ref,
                h_scr, halo_scr, act_scr, *, alpha, d_ff, chunk):
    tm = x_ref.shape[1]
    taps = cw_ref.shape[0]
    first = pl.program_id(1) == 0

    @pl.when(first)
    def _():
        halo_scr[...] = jnp.zeros_like(halo_scr)

    x = x_ref[0]
    sh = mod_ref[0, 3:4, :]
    sc = mod_ref[0, 4:5, :]
    h_scr[...] = (x * (1.0 + sc) + sh).astype(BF16)

    def conv_cols(c0):
        u = _dot(h_scr[...], wu_ref[:, c0:c0 + chunk])
        ext = jnp.concatenate([halo_scr[:, c0:c0 + chunk], u], axis=0)
        halo_scr[:, c0:c0 + chunk] = u[tm - HALO:tm, :]
        return _causal_conv_rows(ext, cw_ref[:, c0:c0 + chunk]) + cb_ref[:, c0:c0 + chunk]

    for c0 in range(0, d_ff, chunk):
        gte = conv_cols(c0)
        val = conv_cols(d_ff + c0)
        act_scr[:, c0:c0 + chunk] = (gte * _sigmoid(gte) * val).astype(BF16)

    y = _dot(act_scr[...], wd_ref[...])
    gate = mod_ref[0, 5:6, :]
    r = alpha * x + (1.0 + gate) * y
    o_ref[0] = _layer_norm(r, g_ref[...], b_ref[...])


def _ffn(x, mod3, w_up, conv_w, conv_b, w_down, g, b, alpha, tm=256, chunk=256):
    bsz, s, d = x.shape
    d_ff = w_down.shape[0]
    taps = conv_w.shape[0]
    kern = functools.partial(_ffn_kernel, alpha=alpha, d_ff=d_ff, chunk=chunk)
    return pl.pallas_call(
        kern,
        grid=(bsz, s // tm),
        in_specs=[pl.BlockSpec((1, tm, d), lambda b, i: (b, i, 0)),
                  pl.BlockSpec((1, 6, d), lambda b, i: (b, 0, 0)),
                  pl.BlockSpec((d, 2 * d_ff), lambda b, i: (0, 0), pipeline_mode=pl.Buffered(1)),
                  pl.BlockSpec((taps, 2 * d_ff), lambda b, i: (0, 0)),
                  pl.BlockSpec((1, 2 * d_ff), lambda b, i: (0, 0)),
                  pl.BlockSpec((d_ff, d), lambda b, i: (0, 0), pipeline_mode=pl.Buffered(1)),
                  pl.BlockSpec((1, d), lambda b, i: (0, 0)),
                  pl.BlockSpec((1, d), lambda b, i: (0, 0))],
        out_specs=pl.BlockSpec((1, tm, d), lambda b, i: (b, i, 0)),
        out_shape=jax.ShapeDtypeStruct((bsz, s, d), F32),
        scratch_shapes=[pltpu.VMEM((tm, d), BF16),
                        pltpu.VMEM((HALO, 2 * d_ff), F32),
                        pltpu.VMEM((tm, d_ff), BF16)],
        compiler_params=_cparams(("parallel", "arbitrary")),
        name="ffn_ln",
    )(x, mod3, w_up, conv_w, conv_b.reshape(1, 2 * d_ff), w_down, g.reshape(1, d), b.reshape(1, d))


def _pad_cols(a, width):
    return jnp.pad(a, ((0, 0), (0, width - a.shape[1])))


def kernel(x, c, ada_w, ada_b, mix_in_w, ssm_conv_w, ssm_conv_b, ssm_dt_bias, ssm_a_log, ssm_d, ssm_norm_w,
           mix_out_w, ln1_g, ln1_b, ffn_up_w, ffn_conv_w, ffn_conv_b, ffn_down_w, ln2_g, ln2_b):
    depth = ada_w.shape[0]
    bsz, s, d = x.shape
    assert s % max(MOBA_BLOCK, SSD_CHUNK) == 0
    d_attn = N_ATTN_HEADS * HEAD_DIM
    d_ssm = N_SSM_HEADS * SSM_HEAD_DIM
    d_xbc = d_ssm + 2 * N_SSM_GROUPS * D_STATE
    hpg = N_SSM_HEADS // N_SSM_GROUPS
    alpha = (2.0 * depth) ** 0.25

    slope_vals = 2.0 ** (-8.0 * jnp.arange(1, N_ATTN_HEADS + 1, dtype=F32) / N_ATTN_HEADS)
    slopes = jnp.broadcast_to(slope_vals.reshape(N_ATTN_HEADS // 2, 2, 1), (N_ATTN_HEADS // 2, 2, MOBA_BLOCK))
    key_tab = _attn_key_table(slope_vals, s)

    def per_group_lanes(v):
        return _pad_cols(v.reshape(N_SSM_GROUPS, hpg), LANES).reshape(1, N_SSM_GROUPS * LANES)

    for li in range(depth):
        mod3 = _adaln(c, ada_w[li], ada_b[li]).reshape(bsz, 6, d)

        w_in = mix_in_w[li]
        n_main = 3 * d_attn + d_ssm + d_xbc
        w_dt = w_in[:, n_main:].reshape(d, N_SSM_GROUPS, hpg)
        w_dt = jnp.pad(w_dt, ((0, 0), (0, 0), (0, LANES - hpg))).reshape(d, N_SSM_GROUPS * LANES)
        w_pad = jnp.concatenate([w_in[:, :n_main], w_dt], axis=1).astype(BF16)
        segs = ((0, 3 * d_attn), (3 * d_attn, d_ssm), (3 * d_attn + d_ssm, d_xbc), (n_main, N_SSM_GROUPS * LANES))
        qkv, z, xbc, dt = _inproj(x, mod3, w_pad, segs)

        o_attn = _attention(qkv, key_tab, slopes)
        y_ssm = _ssd(xbc, dt, z, ssm_conv_w[li], ssm_conv_b[li].reshape(1, d_xbc),
                     per_group_lanes(ssm_dt_bias[li]), per_group_lanes(ssm_a_log[li]),
                     jnp.repeat(ssm_d[li], SSM_HEAD_DIM).reshape(1, d_ssm), ssm_norm_w[li].reshape(1, d_ssm))

        x = _outproj(o_attn, y_ssm, x, mod3, mix_out_w[li].astype(BF16), ln1_g[li], ln1_b[li], alpha)
        x = _ffn(x, mod3, ffn_up_w[li].astype(BF16), ffn_conv_w[li], ffn_conv_b[li],
                 ffn_down_w[li].astype(BF16), ln2_g[li], ln2_b[li], alpha)
    return x
```

```python
import functools

import jax
import jax.numpy as jnp
from jax import lax
from jax.experimental import pallas as pl
from jax.experimental.pallas import tpu as pltpu

F32 = jnp.float32
BF16 = jnp.bfloat16

N_ATTN_HEADS = 8
HEAD_DIM = 64
MOBA_BLOCK = 256
MOBA_TOPK = 3
N_SSM_HEADS = 16
SSM_HEAD_DIM = 64
N_SSM_GROUPS = 2
D_STATE = 128
SSD_CHUNK = 256
LN_EPS = 1e-5
RMS_EPS = 1e-5

LANES = 128
HALO = 8
LOG2E = 1.4426950408889634
NEG_BIG = -1e30
VMEM_LIMIT = 52 * 1024 * 1024


def _cparams(sem):
    return pltpu.CompilerParams(dimension_semantics=sem, vmem_limit_bytes=VMEM_LIMIT)


def _sigmoid(x):
    return 1.0 / (1.0 + jnp.exp(-x))


def _dot(a, b):
    return jnp.dot(a, b, preferred_element_type=F32)


def _dot_nt(a, b):
    return lax.dot_general(a, b, (((1,), (1,)), ((), ())), preferred_element_type=F32)


def _causal_conv_rows(ext, w):
    taps = w.shape[0]
    acc = w[0:1, :] * ext
    for j in range(1, taps):
        acc = pltpu.roll(acc, 1, axis=0) + w[j:j + 1, :] * ext
    return acc[HALO:, :]


def _layer_norm(r, g, b):
    mu = jnp.mean(r, axis=-1, keepdims=True)
    d = r - mu
    var = jnp.mean(d * d, axis=-1, keepdims=True)
    return d * lax.rsqrt(var + LN_EPS) * g + b


def _adaln_kernel(c_ref, w_ref, b_ref, o_ref):
    c = c_ref[...]
    ca = (c * _sigmoid(c)).astype(BF16)
    o_ref[...] = _dot(ca, w_ref[...].astype(BF16)) + b_ref[...]


def _adaln(c, w, b):
    bsz, d = c.shape
    n = w.shape[1]
    tn = d
    return pl.pallas_call(
        _adaln_kernel,
        grid=(n // tn,),
        in_specs=[pl.BlockSpec((bsz, d), lambda j: (0, 0)),
                  pl.BlockSpec((d, tn), lambda j: (0, j)),
                  pl.BlockSpec((1, tn), lambda j: (0, j))],
        out_specs=pl.BlockSpec((bsz, tn), lambda j: (0, j)),
        out_shape=jax.ShapeDtypeStruct((bsz, n), F32),
        compiler_params=_cparams(("parallel",)),
        name="adaln",
    )(c, w, b.reshape(1, n))


def _inproj_kernel(x_ref, mod_ref, w_ref, qkv_ref, z_ref, xbc_ref, dt_ref, h_scr, *, segs, chunk):
    sh = mod_ref[0, 0:1, :]
    sc = mod_ref[0, 1:2, :]
    h_scr[...] = (x_ref[0] * (1.0 + sc) + sh).astype(BF16)
    (c0, width) = segs[0]
    for a in range(0, width, chunk):
        res = _dot(h_scr[...], w_ref[:, c0 + a:c0 + a + chunk]).astype(qkv_ref.dtype)
        for p in range(chunk // LANES):
            qkv_ref[0, a // LANES + p] = res[:, p * LANES:(p + 1) * LANES]
    for dst, (c0, width) in zip((z_ref, xbc_ref, dt_ref), segs[1:]):
        for a in range(0, width, chunk):
            n = min(chunk, width - a)
            dst[0, :, a:a + n] = _dot(h_scr[...], w_ref[:, c0 + a:c0 + a + n]).astype(dst.dtype)


def _inproj(x, mod3, w_pad, segs, tm=512, chunk=512):
    bsz, s, d = x.shape
    ntot = w_pad.shape[1]
    widths = [w for _, w in segs]
    assert widths[0] % chunk == 0
    nslab = widths[0] // LANES
    kern = functools.partial(_inproj_kernel, segs=segs, chunk=chunk)
    return pl.pallas_call(
        kern,
        grid=(bsz, s // tm),
        in_specs=[pl.BlockSpec((1, tm, d), lambda b, i: (b, i, 0)),
                  pl.BlockSpec((1, 6, d), lambda b, i: (b, 0, 0)),
                  pl.BlockSpec((d, ntot), lambda b, i: (0, 0), pipeline_mode=pl.Buffered(1))],
        out_specs=[pl.BlockSpec((1, nslab, tm, LANES), lambda b, i: (b, 0, i, 0))]
        + [pl.BlockSpec((1, tm, w), lambda b, i: (b, i, 0)) for w in widths[1:]],
        out_shape=[jax.ShapeDtypeStruct((bsz, nslab, s, LANES), BF16)]
        + [jax.ShapeDtypeStruct((bsz, s, w), F32) for w in widths[1:]],
        scratch_shapes=[pltpu.VMEM((tm, d), BF16)],
        compiler_params=_cparams(("parallel", "parallel")),
        name="inproj",
    )(x, mod3, w_pad)


def _attn_kernel(q_ref, k_ref, v_ref, tab_ref, slope_ref, o_ref,
                 kt_scr, va_scr, kmean_scr, *, nb, npairs, scale):
    blk = MOBA_BLOCK
    i = pl.program_id(1)
    lane = lax.broadcasted_iota(jnp.int32, (blk, LANES), 1)

    @pl.when(i == 0)
    def _():
        kmean_scr[...] = jnp.zeros_like(kmean_scr)

        def prep(pr, carry):
            for j in range(nb):
                kj = k_ref[0, pr, j * blk:(j + 1) * blk, :]
                vj = v_ref[0, pr, j * blk:(j + 1) * blk, :]
                kmean_scr[pr, j:j + 1, :] = jnp.mean(kj.astype(F32), axis=0, keepdims=True)
                for hh in range(2):
                    own = (lane // HEAD_DIM) == hh
                    k_aug = jnp.where(own, kj, tab_ref[2 * pr + hh, j * blk:(j + 1) * blk, :])
                    kt_scr[2 * pr + hh, :, j * blk:(j + 1) * blk] = k_aug.astype(F32).T.astype(BF16)
                    va_scr[2 * pr + hh, j * blk:(j + 1) * blk, :] = jnp.where(own, vj, jnp.ones_like(vj))
            return carry

        lax.fori_loop(0, npairs, prep, 0)

    row = lax.broadcasted_iota(jnp.int32, (blk, blk), 0)
    col = lax.broadcasted_iota(jnp.int32, (blk, blk), 1)
    causal = row >= col
    blkid = lax.broadcasted_iota(jnp.int32, (8, blk), 0)
    past = blkid < i
    ones_t = jnp.where(blkid < 2, 1.0, 0.0).astype(F32)
    dist = ((i - blkid) * blk).astype(F32)

    q_augs = []
    for pr in range(npairs):
        q = q_ref[0, pr]
        q_scaled = q * jnp.asarray(scale, BF16)
        kmean = kmean_scr[pr].astype(BF16)
        for hh in range(2):
            own = (lane // HEAD_DIM) == hh
            qh = jnp.where(own, q, jnp.zeros_like(q))

            gate = _dot_nt(kmean, qh)[0:8, :]
            gate = jnp.where(past, gate, -jnp.inf)
            rank = jnp.zeros((8, blk), F32)
            for jp in range(nb):
                gj = gate[jp:jp + 1, :]
                beats = (gj > gate) | ((gj == gate) & (blkid > jp))
                rank = rank + jnp.where(beats, 1.0, 0.0)
            sel = (rank < MOBA_TOPK) & past
            slope = slope_ref[pr, hh:hh + 1, :]
            bias_t = jnp.where(sel, -slope * dist, NEG_BIG)
            bias_t = jnp.where(blkid == i, 0.0, bias_t)
            pieces = [bias_t, ones_t, jnp.zeros((HEAD_DIM - 16, blk), F32)]
            zeros_own = jnp.zeros((HEAD_DIM, blk), F32)
            pieces = [zeros_own] + pieces if hh == 0 else pieces + [zeros_own]
            extra = jnp.concatenate(pieces, axis=0).T
            q_augs.append(jnp.where(own, q_scaled, extra.astype(BF16)))

    def attend(n):
        for pr in range(npairs):
            outs = []
            for hh in range(2):
                h = 2 * pr + hh
                s = _dot(q_augs[h], kt_scr[h, :, 0:n * blk])
                parts = [s[:, j * blk:(j + 1) * blk] for j in range(n)]
                parts[-1] = jnp.where(causal, parts[-1], NEG_BIG)
                m = parts[-1]
                for part in parts[:-1]:
                    m = jnp.maximum(m, part)
                m = jnp.max(m, axis=-1, keepdims=True)
                p = jnp.concatenate([jnp.exp(part - m).astype(BF16) for part in parts], axis=1)
                acc = _dot(p, va_scr[h, 0:n * blk, :])
                outs.append(acc / pltpu.roll(acc, HEAD_DIM, axis=1))
            o_pair = jnp.where((lane // HEAD_DIM) == 0, outs[0], outs[1])
            o_ref[0, :, pr * LANES:(pr + 1) * LANES] = o_pair.astype(o_ref.dtype)

    for n in range(1, nb + 1):
        pl.when(i == n - 1)(functools.partial(attend, n))


def _attention(qkv, tab, slopes):
    bsz, nslab, s, _ = qkv.shape
    blk = MOBA_BLOCK
    nb = s // blk
    assert nb <= 8
    npairs = N_ATTN_HEADS // 2
    assert nslab == 3 * npairs
    kern = functools.partial(_attn_kernel, nb=nb, npairs=npairs, scale=HEAD_DIM ** -0.5)
    return pl.pallas_call(
        kern,
        grid=(bsz, nb),
        in_specs=[pl.BlockSpec((1, npairs, blk, LANES), lambda b, i: (b, 0, i, 0)),
                  pl.BlockSpec((1, npairs, s, LANES), lambda b, i: (b, 1, 0, 0)),
                  pl.BlockSpec((1, npairs, s, LANES), lambda b, i: (b, 2, 0, 0)),
                  pl.BlockSpec((N_ATTN_HEADS, s, LANES), lambda b, i: (0, 0, 0), pipeline_mode=pl.Buffered(1)),
                  pl.BlockSpec((npairs, 2, blk), lambda b, i: (0, 0, 0))],
        out_specs=pl.BlockSpec((1, blk, npairs * LANES), lambda b, i: (b, i, 0)),
        out_shape=jax.ShapeDtypeStruct((bsz, s, npairs * LANES), BF16),
        scratch_shapes=[pltpu.VMEM((N_ATTN_HEADS, LANES, s), BF16),
                        pltpu.VMEM((N_ATTN_HEADS, s, LANES), BF16),
                        pltpu.VMEM((npairs, 16, LANES), F32)],
        compiler_params=_cparams(("parallel", "arbitrary")),
        name="moba_attn",
    )(qkv, qkv, qkv, tab, slopes)


def _attn_key_table(slope_vals, s):
    pos = jnp.arange(s)
    blk_of = pos // MOBA_BLOCK
    within = (pos % MOBA_BLOCK).astype(F32)
    tabs = []
    for h in range(N_ATTN_HEADS):
        ind = (blk_of[:, None] == jnp.arange(8)[None, :]).astype(F32)
        val = slope_vals[h] * within
        hi = val.astype(BF16).astype(F32)
        lo = val - hi
        spare = jnp.concatenate([ind, hi[:, None], lo[:, None], jnp.zeros((s, HEAD_DIM - 10), F32)], axis=1)
        own = jnp.zeros((s, HEAD_DIM), F32)
        tabs.append(jnp.concatenate([own, spare] if h % 2 == 0 else [spare, own], axis=1))
    return jnp.stack(tabs).astype(BF16)


def _conv_silu(x_ref, halo_ref, w_ref, b_ref, first):
    rows = x_ref.shape[1]

    @pl.when(first)
    def _():
        halo_ref[...] = jnp.zeros_like(halo_ref)

    ext = jnp.concatenate([halo_ref[...], x_ref[0]], axis=0)
    halo_ref[...] = x_ref[0, rows - HALO:rows, :]
    acc = _causal_conv_rows(ext, w_ref[...]) + b_ref[...]
    return acc * _sigmoid(acc)


def _ssd_kernel(xs_ref, bm_ref, cm_ref, dt_ref, z_ref,
                wx_ref, wb_ref, wc_ref, bx_ref, bb_ref, bc_ref,
                dtb_ref, alog_ref, dskip_ref, nw_ref,
                y_ref,
                hx_scr, hb_scr, hc_scr, state_scr, xdec_scr, ydiag_scr):
    L = SSD_CHUNK
    hpg = N_SSM_HEADS // N_SSM_GROUPS
    first = pl.program_id(2) == 0

    xs = _conv_silu(xs_ref, hx_scr, wx_ref, bx_ref, first)
    bm = _conv_silu(bm_ref, hb_scr, wb_ref, bb_ref, first)
    cm = _conv_silu(cm_ref, hc_scr, wc_ref, bc_ref, first)

    @pl.when(first)
    def _():
        state_scr[...] = jnp.zeros_like(state_scr)

    xdt_in = dt_ref[0] + dtb_ref[...]
    dt = jnp.maximum(xdt_in, 0.0) + jnp.log1p(jnp.exp(-jnp.abs(xdt_in)))
    a2 = -jnp.exp(alog_ref[...]) * LOG2E
    da = dt * a2
    row = lax.broadcasted_iota(jnp.int32, (L, L), 0)
    col = lax.broadcasted_iota(jnp.int32, (L, L), 1)
    causal = row >= col
    tri = jnp.where(causal, 1.0, 0.0).astype(BF16)
    d_hi = da.astype(BF16)
    rem = da - d_hi.astype(F32)
    d_mid = rem.astype(BF16)
    d_lo = (rem - d_mid.astype(F32)).astype(BF16)
    csp = _dot(tri, jnp.concatenate([d_hi, d_mid, d_lo], axis=1))
    cs = (csp[:, 2 * LANES:] + csp[:, LANES:2 * LANES]) + csp[:, :LANES]
    cs_t = cs.T
    cs_last = cs[L - 1:L, :]

    bm16 = bm.astype(BF16)
    cm16 = cm.astype(BF16)
    cb = _dot_nt(cm16, bm16)
    lane = lax.broadcasted_iota(jnp.int32, (L, LANES), 1)
    lo_half = lane < SSM_HEAD_DIM

    cd_parts, eo_parts, xdt_parts = [], [], []
    for pr in range(hpg // 2):
        h0, h1 = 2 * pr, 2 * pr + 1
        xs_p = xs[:, pr * LANES:(pr + 1) * LANES]
        cs_p = jnp.where(lo_half, cs[:, h0:h0 + 1], cs[:, h1:h1 + 1])
        dt_p = jnp.where(lo_half, dt[:, h0:h0 + 1], dt[:, h1:h1 + 1])
        csl_p = jnp.where(lo_half[0:1], cs_last[:, h0:h0 + 1], cs_last[:, h1:h1 + 1])
        xdt_p = xs_p * dt_p
        xdt16 = xdt_p.astype(BF16)
        xdec_scr[:, pr * LANES:(pr + 1) * LANES] = (xdt_p * jnp.exp2(csl_p - cs_p)).astype(BF16)
        cd_parts.append(jnp.exp2(csl_p))
        eo_parts.append(jnp.exp2(cs_p))
        ys = []
        for h in (h0, h1):
            diff = cs[:, h:h + 1] - cs_t[h:h + 1, :]
            lmat = jnp.exp2(jnp.where(causal, diff, -jnp.inf))
            ys.append(_dot((cb * lmat).astype(BF16), xdt16))
        ydiag_scr[:, pr * LANES:(pr + 1) * LANES] = jnp.where(lo_half, ys[0], ys[1])

    eo = jnp.concatenate(eo_parts, axis=-1)
    cd = jnp.concatenate(cd_parts, axis=-1)
    state = state_scr[...]
    y = ydiag_scr[...] + eo * _dot(cm16, state.astype(BF16)) + xs * dskip_ref[...]
    state_scr[...] = state * cd + _dot(bm.T.astype(BF16), xdec_scr[...])

    z = z_ref[0]
    y = y * (z * _sigmoid(z))
    y = y * lax.rsqrt(jnp.mean(y * y, axis=-1, keepdims=True) + RMS_EPS) * nw_ref[...]
    y_ref[0] = y.astype(y_ref.dtype)


def _ssd(xbc, dt, z, conv_w, conv_b, dtb, alog, dskip, norm_w):
    bsz, s, _ = xbc.shape
    L = SSD_CHUNK
    nc = s // L
    G = N_SSM_GROUPS
    gw = (N_SSM_HEADS // G) * SSM_HEAD_DIM
    d_ssm = N_SSM_HEADS * SSM_HEAD_DIM
    taps = conv_w.shape[0]
    xo = gw // LANES
    bo = d_ssm // LANES
    co = bo + G * D_STATE // LANES

    def seq(width, off_fn):
        return pl.BlockSpec((1, L, width), lambda b, g, c: (b, c, off_fn(g)))

    def par(rows, width, off_fn):
        return pl.BlockSpec((rows, width), lambda b, g, c: (0, off_fn(g)))

    in_specs = [
        seq(gw, lambda g: g), seq(D_STATE, lambda g: bo + g), seq(D_STATE, lambda g: co + g),
        seq(LANES, lambda g: g), seq(gw, lambda g: g),
        par(taps, gw, lambda g: g), par(taps, D_STATE, lambda g: bo + g), par(taps, D_STATE, lambda g: co + g),
        par(1, gw, lambda g: g), par(1, D_STATE, lambda g: bo + g), par(1, D_STATE, lambda g: co + g),
        par(1, LANES, lambda g: g), par(1, LANES, lambda g: g),
        par(1, gw, lambda g: g), par(1, gw, lambda g: g),
    ]
    return pl.pallas_call(
        _ssd_kernel,
        grid=(bsz, G, nc),
        in_specs=in_specs,
        out_specs=pl.BlockSpec((1, L, gw), lambda b, g, c: (b, c, g)),
        out_shape=jax.ShapeDtypeStruct((bsz, s, d_ssm), BF16),
        scratch_shapes=[pltpu.VMEM((HALO, gw), F32), pltpu.VMEM((HALO, D_STATE), F32), pltpu.VMEM((HALO, D_STATE), F32),
                        pltpu.VMEM((D_STATE, gw), F32), pltpu.VMEM((L, gw), BF16), pltpu.VMEM((L, gw), F32)],
        compiler_params=_cparams(("parallel", "parallel", "arbitrary")),
        name="ssd",
    )(xbc, xbc, xbc, dt, z, conv_w, conv_w, conv_w, conv_b, conv_b, conv_b, dtb, alog, dskip, norm_w)


def _outproj_kernel(oa_ref, y_ref, x_ref, mod_ref, w_ref, g_ref, b_ref, o_ref, *, alpha, d_attn):
    acc = _dot(oa_ref[0], w_ref[0:d_attn, :]) + _dot(y_ref[0], w_ref[d_attn:, :])
    gate = mod_ref[0, 2:3, :]
    r = alpha * x_ref[0] + (1.0 + gate) * acc
    o_ref[0] = _layer_norm(r, g_ref[...], b_ref[...])


def _outproj(oa, y, x, mod3, w, g, b, alpha, tm=512):
    bsz, s, d = x.shape
    d_attn = oa.shape[-1]
    d_ssm = y.shape[-1]
    kern = functools.partial(_outproj_kernel, alpha=alpha, d_attn=d_attn)
    return pl.pallas_call(
        kern,
        grid=(bsz, s // tm),
        in_specs=[pl.BlockSpec((1, tm, d_attn), lambda b, i: (b, i, 0)),
                  pl.BlockSpec((1, tm, d_ssm), lambda b, i: (b, i, 0)),
                  pl.BlockSpec((1, tm, d), lambda b, i: (b, i, 0)),
                  pl.BlockSpec((1, 6, d), lambda b, i: (b, 0, 0)),
                  pl.BlockSpec((d_attn + d_ssm, d), lambda b, i: (0, 0), pipeline_mode=pl.Buffered(1)),
                  pl.BlockSpec((1, d), lambda b, i: (0, 0)),
                  pl.BlockSpec((1, d), lambda b, i: (0, 0))],
        out_specs=pl.BlockSpec((1, tm, d), lambda b, i: (b, i, 0)),
        out_shape=jax.ShapeDtypeStruct((bsz, s, d), F32),
        compiler_params=_cparams(("parallel", "parallel")),
        name="outproj_ln",
    )(oa, y, x, mod3, w, g.reshape(1, d), b.reshape(1, d))


def _ffn_kernel(x_ref, mod_ref, wu_ref, cw_ref, cb_ref, wd_ref, g_ref, b_ref, o_ref,
                h_scr, halo_scr, act_scr, *, alpha, d_ff, chunk):
    tm = x_ref.shape[1]
    taps = cw_ref.shape[0]
    first = pl.program_id(1) == 0

    @pl.when(first)
    def _():
        halo_scr[...] = jnp.zeros_like(halo_scr)

    x = x_ref[0]
    sh = mod_ref[0, 3:4, :]
    sc = mod_ref[0, 4:5, :]
    h_scr[...] = (x * (1.0 + sc) + sh).astype(BF16)

    def conv_cols(c0):
        u = _dot(h_scr[...], wu_ref[:, c0:c0 + chunk])
        ext = jnp.concatenate([halo_scr[:, c0:c0 + chunk], u], axis=0)
        halo_scr[:, c0:c0 + chunk] = u[tm - HALO:tm, :]
        return _causal_conv_rows(ext, cw_ref[:, c0:c0 + chunk]) + cb_ref[:, c0:c0 + chunk]

    for c0 in range(0, d_ff, chunk):
        gte = conv_cols(c0)
        val = conv_cols(d_ff + c0)
        act_scr[:, c0:c0 + chunk] = (gte * _sigmoid(gte) * val).astype(BF16)

    y = _dot(act_scr[...], wd_ref[...])
    gate = mod_ref[0, 5:6, :]
    r = alpha * x + (1.0 + gate) * y
    o_ref[0] = _layer_norm(r, g_ref[...], b_ref[...])


def _ffn(x, mod3, w_up, conv_w, conv_b, w_down, g, b, alpha, tm=256, chunk=256):
    bsz, s, d = x.shape
    d_ff = w_down.shape[0]
    taps = conv_w.shape[0]
    kern = functools.partial(_ffn_kernel, alpha=alpha, d_ff=d_ff, chunk=chunk)
    return pl.pallas_call(
        kern,
        grid=(bsz, s // tm),
        in_specs=[pl.BlockSpec((1, tm, d), lambda b, i: (b, i, 0)),
                  pl.BlockSpec((1, 6, d), lambda b, i: (b, 0, 0)),
                  pl.BlockSpec((d, 2 * d_ff), lambda b, i: (0, 0), pipeline_mode=pl.Buffered(1)),
                  pl.BlockSpec((taps, 2 * d_ff), lambda b, i: (0, 0)),
                  pl.BlockSpec((1, 2 * d_ff), lambda b, i: (0, 0)),
                  pl.BlockSpec((d_ff, d), lambda b, i: (0, 0), pipeline_mode=pl.Buffered(1)),
                  pl.BlockSpec((1, d), lambda b, i: (0, 0)),
                  pl.BlockSpec((1, d), lambda b, i: (0, 0))],
        out_specs=pl.BlockSpec((1, tm, d), lambda b, i: (b, i, 0)),
        out_shape=jax.ShapeDtypeStruct((bsz, s, d), F32),
        scratch_shapes=[pltpu.VMEM((tm, d), BF16),
                        pltpu.VMEM((HALO, 2 * d_ff), F32),
                        pltpu.VMEM((tm, d_ff), BF16)],
        compiler_params=_cparams(("parallel", "arbitrary")),
        name="ffn_ln",
    )(x, mod3, w_up, conv_w, conv_b.reshape(1, 2 * d_ff), w_down, g.reshape(1, d), b.reshape(1, d))


def _pad_cols(a, width):
    return jnp.pad(a, ((0, 0), (0, width - a.shape[1])))


def kernel(x, c, ada_w, ada_b, mix_in_w, ssm_conv_w, ssm_conv_b, ssm_dt_bias, ssm_a_log, ssm_d, ssm_norm_w,
           mix_out_w, ln1_g, ln1_b, ffn_up_w, ffn_conv_w, ffn_conv_b, ffn_down_w, ln2_g, ln2_b):
    depth = ada_w.shape[0]
    bsz, s, d = x.shape
    assert s % max(MOBA_BLOCK, SSD_CHUNK) == 0
    d_attn = N_ATTN_HEADS * HEAD_DIM
    d_ssm = N_SSM_HEADS * SSM_HEAD_DIM
    d_xbc = d_ssm + 2 * N_SSM_GROUPS * D_STATE
    hpg = N_SSM_HEADS // N_SSM_GROUPS
    alpha = (2.0 * depth) ** 0.25

    slope_vals = 2.0 ** (-8.0 * jnp.arange(1, N_ATTN_HEADS + 1, dtype=F32) / N_ATTN_HEADS)
    slopes = jnp.broadcast_to(slope_vals.reshape(N_ATTN_HEADS // 2, 2, 1), (N_ATTN_HEADS // 2, 2, MOBA_BLOCK))
    key_tab = _attn_key_table(slope_vals, s)

    def per_group_lanes(v):
        return _pad_cols(v.reshape(N_SSM_GROUPS, hpg), LANES).reshape(1, N_SSM_GROUPS * LANES)

    for li in range(depth):
        mod3 = _adaln(c, ada_w[li], ada_b[li]).reshape(bsz, 6, d)

        w_in = mix_in_w[li]
        n_main = 3 * d_attn + d_ssm + d_xbc
        w_dt = w_in[:, n_main:].reshape(d, N_SSM_GROUPS, hpg)
        w_dt = jnp.pad(w_dt, ((0, 0), (0, 0), (0, LANES - hpg))).reshape(d, N_SSM_GROUPS * LANES)
        w_pad = jnp.concatenate([w_in[:, :n_main], w_dt], axis=1).astype(BF16)
        segs = ((0, 3 * d_attn), (3 * d_attn, d_ssm), (3 * d_attn + d_ssm, d_xbc), (n_main, N_SSM_GROUPS * LANES))
        qkv, z, xbc, dt = _inproj(x, mod3, w_pad, segs)

        o_attn = _attention(qkv, key_tab, slopes)
        y_ssm = _ssd(xbc, dt, z, ssm_conv_w[li], ssm_conv_b[li].reshape(1, d_xbc),
                     per_group_lanes(ssm_dt_bias[li]), per_group_lanes(ssm_a_log[li]),
                     jnp.repeat(ssm_d[li], SSM_HEAD_DIM).reshape(1, d_ssm), ssm_norm_w[li].reshape(1, d_ssm))

        x = _outproj(o_attn, y_ssm, x, mod3, mix_out_w[li].astype(BF16), ln1_g[li], ln1_b[li], alpha)
        x = _ffn(x, mod3, ffn_up_w[li].astype(BF16), ffn_conv_w[li], ffn_conv_b[li],
                 ffn_down_w[li].astype(BF16), ln2_g[li], ln2_b[li], alpha)
    return x
```

```python
import functools

import jax
import jax.numpy as jnp
from jax import lax
from jax.experimental import pallas as pl
from jax.experimental.pallas import tpu as pltpu

F32 = jnp.float32
BF16 = jnp.bfloat16

N_ATTN_HEADS = 8
HEAD_DIM = 64
MOBA_BLOCK = 256
MOBA_TOPK = 3
N_SSM_HEADS = 16
SSM_HEAD_DIM = 64
N_SSM_GROUPS = 2
D_STATE = 128
SSD_CHUNK = 256
LN_EPS = 1e-5
RMS_EPS = 1e-5

LANES = 128
HALO = 8
LOG2E = 1.4426950408889634
NEG_BIG = -1e30
VMEM_LIMIT = 52 * 1024 * 1024


def _cparams(sem):
    return pltpu.CompilerParams(dimension_semantics=sem, vmem_limit_bytes=VMEM_LIMIT)


def _sigmoid(x):
    return 1.0 / (1.0 + jnp.exp(-x))


def _dot(a, b):
    return jnp.dot(a, b, preferred_element_type=F32)


def _dot_nt(a, b):
    return lax.dot_general(a, b, (((1,), (1,)), ((), ())), preferred_element_type=F32)


def _causal_conv_rows(ext, w):
    taps = w.shape[0]
    acc = w[0:1, :] * ext
    for j in range(1, taps):
        acc = pltpu.roll(acc, 1, axis=0) + w[j:j + 1, :] * ext
    return acc[HALO:, :]


def _layer_norm(r, g, b):
    mu = jnp.mean(r, axis=-1, keepdims=True)
    d = r - mu
    var = jnp.mean(d * d, axis=-1, keepdims=True)
    return d * lax.rsqrt(var + LN_EPS) * g + b


def _adaln_kernel(c_ref, w_ref, b_ref, o_ref):
    c = c_ref[...]
    ca = (c * _sigmoid(c)).astype(BF16)
    o_ref[...] = _dot(ca, w_ref[...].astype(BF16)) + b_ref[...]


def _adaln(c, w, b):
    bsz, d = c.shape
    n = w.shape[1]
    tn = d
    return pl.pallas_call(
        _adaln_kernel,
        grid=(n // tn,),
        in_specs=[pl.BlockSpec((bsz, d), lambda j: (0, 0)),
                  pl.BlockSpec((d, tn), lambda j: (0, j)),
                  pl.BlockSpec((1, tn), lambda j: (0, j))],
        out_specs=pl.BlockSpec((bsz, tn), lambda j: (0, j)),
        out_shape=jax.ShapeDtypeStruct((bsz, n), F32),
        compiler_params=_cparams(("parallel",)),
        name="adaln",
    )(c, w, b.reshape(1, n))


def _inproj_kernel(x_ref, mod_ref, w_ref, cw_ref, cb_ref, qkv_ref, sz_ref, xs_ref, bc_ref, dt_ref,
                   h_scr, halo_scr, *, segs, chunk):
    tm = x_ref.shape[1]

    @pl.when(pl.program_id(1) == 0)
    def _():
        halo_scr[...] = jnp.zeros_like(halo_scr)

    sh = mod_ref[0, 0:1, :]
    sc = mod_ref[0, 1:2, :]
    h_scr[...] = (x_ref[0] * (1.0 + sc) + sh).astype(BF16)

    def cols(c0, n):
        return _dot(h_scr[...], w_ref[:, c0:c0 + n])

    (c0, width) = segs[2]
    d_x = xs_ref.shape[-1]
    for a in range(0, width, chunk):
        u = cols(c0 + a, chunk)
        ext = jnp.concatenate([halo_scr[:, a:a + chunk], u], axis=0)
        halo_scr[:, a:a + chunk] = u[tm - HALO:tm, :]
        act = _causal_conv_rows(ext, cw_ref[:, a:a + chunk]) + cb_ref[:, a:a + chunk]
        act = act * _sigmoid(act)
        if a < d_x:
            xs_ref[0, :, a:a + chunk] = act
        else:
            bc_ref[0, :, a - d_x:a - d_x + chunk] = act.astype(bc_ref.dtype)

    (c0, width) = segs[1]
    for a in range(0, width, chunk):
        zc = cols(c0 + a, chunk)
        sz_ref[0, :, a:a + chunk] = zc * _sigmoid(zc)

    (c0, width) = segs[0]
    for a in range(0, width, chunk):
        res = cols(c0 + a, chunk).astype(qkv_ref.dtype)
        for p in range(chunk // LANES):
            qkv_ref[0, a // LANES + p] = res[:, p * LANES:(p + 1) * LANES]

    (c0, width) = segs[3]
    dt_ref[0] = cols(c0, width)


def _inproj(x, mod3, w_pad, conv_w, conv_b, segs, d_x, tm=512, chunk=512):
    bsz, s, d = x.shape
    ntot = w_pad.shape[1]
    widths = [w for _, w in segs]
    assert all(w % chunk == 0 for w in widths[:3]) and d_x % chunk == 0
    nslab = widths[0] // LANES
    d_bc = widths[2] - d_x
    taps = conv_w.shape[0]
    kern = functools.partial(_inproj_kernel, segs=segs, chunk=chunk)

    def rows(width):
        return pl.BlockSpec((1, tm, width), lambda b, i: (b, i, 0))

    return pl.pallas_call(
        kern,
        grid=(bsz, s // tm),
        in_specs=[rows(d),
                  pl.BlockSpec((1, 6, d), lambda b, i: (b, 0, 0)),
                  pl.BlockSpec((d, ntot), lambda b, i: (0, 0), pipeline_mode=pl.Buffered(1)),
                  pl.BlockSpec((taps, widths[2]), lambda b, i: (0, 0)),
                  pl.BlockSpec((1, widths[2]), lambda b, i: (0, 0))],
        out_specs=[pl.BlockSpec((1, nslab, tm, LANES), lambda b, i: (b, 0, i, 0)),
                   rows(widths[1]), rows(d_x), rows(d_bc), rows(widths[3])],
        out_shape=[jax.ShapeDtypeStruct((bsz, nslab, s, LANES), BF16),
                   jax.ShapeDtypeStruct((bsz, s, widths[1]), F32),
                   jax.ShapeDtypeStruct((bsz, s, d_x), F32),
                   jax.ShapeDtypeStruct((bsz, s, d_bc), BF16),
                   jax.ShapeDtypeStruct((bsz, s, widths[3]), F32)],
        scratch_shapes=[pltpu.VMEM((tm, d), BF16), pltpu.VMEM((HALO, widths[2]), F32)],
        compiler_params=_cparams(("parallel", "arbitrary")),
        name="inproj",
    )(x, mod3, w_pad, conv_w, conv_b)


def _attn_kernel(q_ref, k_ref, v_ref, tab_ref, slope_ref, o_ref,
                 kt_scr, va_scr, kmean_scr, *, nb, npairs, scale):
    blk = MOBA_BLOCK
    i = pl.program_id(1)
    lane = lax.broadcasted_iota(jnp.int32, (blk, LANES), 1)

    @pl.when(i == 0)
    def _():
        kmean_scr[...] = jnp.zeros_like(kmean_scr)

        def prep(pr, carry):
            for j in range(nb):
                kj = k_ref[0, pr, j * blk:(j + 1) * blk, :]
                vj = v_ref[0, pr, j * blk:(j + 1) * blk, :]
                kmean_scr[pr, j:j + 1, :] = jnp.mean(kj.astype(F32), axis=0, keepdims=True)
                for hh in range(2):
                    own = (lane // HEAD_DIM) == hh
                    k_aug = jnp.where(own, kj, tab_ref[2 * pr + hh, j * blk:(j + 1) * blk, :])
                    kt_scr[2 * pr + hh, :, j * blk:(j + 1) * blk] = k_aug.astype(F32).T.astype(BF16)
                    va_scr[2 * pr + hh, j * blk:(j + 1) * blk, :] = jnp.where(own, vj, jnp.ones_like(vj))
            return carry

        lax.fori_loop(0, npairs, prep, 0)

    row = lax.broadcasted_iota(jnp.int32, (blk, blk), 0)
    col = lax.broadcasted_iota(jnp.int32, (blk, blk), 1)
    causal = row >= col
    blkid = lax.broadcasted_iota(jnp.int32, (8, blk), 0)
    past = blkid < i
    ones_t = jnp.where(blkid < 2, 1.0, 0.0).astype(F32)
    dist = ((i - blkid) * blk).astype(F32)

    q_augs = []
    for pr in range(npairs):
        q = q_ref[0, pr]
        q_scaled = q * jnp.asarray(scale, BF16)
        kmean = kmean_scr[pr].astype(BF16)
        for hh in range(2):
            own = (lane // HEAD_DIM) == hh
            qh = jnp.where(own, q, jnp.zeros_like(q))

            gate = _dot_nt(kmean, qh)[0:8, :]
            gate = jnp.where(past, gate, -jnp.inf)
            rank = jnp.zeros((8, blk), F32)
            for jp in range(nb):
                gj = gate[jp:jp + 1, :]
                beats = (gj > gate) | ((gj == gate) & (blkid > jp))
                rank = rank + jnp.where(beats, 1.0, 0.0)
            sel = (rank < MOBA_TOPK) & past
            slope = slope_ref[pr, hh:hh + 1, :]
            bias_t = jnp.where(sel, -slope * dist, NEG_BIG)
            bias_t = jnp.where(blkid == i, 0.0, bias_t)
            pieces = [bias_t, ones_t, jnp.zeros((HEAD_DIM - 16, blk), F32)]
            zeros_own = jnp.zeros((HEAD_DIM, blk), F32)
            pieces = [zeros_own] + pieces if hh == 0 else pieces + [zeros_own]
            extra = jnp.concatenate(pieces, axis=0).T
            q_augs.append(jnp.where(own, q_scaled, extra.astype(BF16)))

    def attend(n):
        for pr in range(npairs):
            outs = []
            for hh in range(2):
                h = 2 * pr + hh
                s = _dot(q_augs[h], kt_scr[h, :, 0:n * blk])
                parts = [s[:, j * blk:(j + 1) * blk] for j in range(n)]
                parts[-1] = jnp.where(causal, parts[-1], NEG_BIG)
                m = parts[-1]
                for part in parts[:-1]:
                    m = jnp.maximum(m, part)
                m = jnp.max(m, axis=-1, keepdims=True)
                p = jnp.concatenate([jnp.exp(part - m).astype(BF16) for part in parts], axis=1)
                acc = _dot(p, va_scr[h, 0:n * blk, :])
                outs.append(acc / pltpu.roll(acc, HEAD_DIM, axis=1))
            o_pair = jnp.where((lane // HEAD_DIM) == 0, outs[0], outs[1])
            o_ref[0, :, pr * LANES:(pr + 1) * LANES] = o_pair.astype(o_ref.dtype)

    for n in range(1, nb + 1):
        pl.when(i == n - 1)(functools.partial(attend, n))


def _attention(qkv, tab, slopes):
    bsz, nslab, s, _ = qkv.shape
    blk = MOBA_BLOCK
    nb = s // blk
    assert nb <= 8
    npairs = N_ATTN_HEADS // 2
    assert nslab == 3 * npairs
    kern = functools.partial(_attn_kernel, nb=nb, npairs=npairs, scale=HEAD_DIM ** -0.5)
    return pl.pallas_call(
        kern,
        grid=(bsz, nb),
        in_specs=[pl.BlockSpec((1, npairs, blk, LANES), lambda b, i: (b, 0, i, 0)),
                  pl.BlockSpec((1, npairs, s, LANES), lambda b, i: (b, 1, 0, 0)),
                  pl.BlockSpec((1, npairs, s, LANES), lambda b, i: (b, 2, 0, 0)),
                  pl.BlockSpec((N_ATTN_HEADS, s, LANES), lambda b, i: (0, 0, 0), pipeline_mode=pl.Buffered(1)),
                  pl.BlockSpec((npairs, 2, blk), lambda b, i: (0, 0, 0))],
        out_specs=pl.BlockSpec((1, blk, npairs * LANES), lambda b, i: (b, i, 0)),
        out_shape=jax.ShapeDtypeStruct((bsz, s, npairs * LANES), BF16),
        scratch_shapes=[pltpu.VMEM((N_ATTN_HEADS, LANES, s), BF16),
                        pltpu.VMEM((N_ATTN_HEADS, s, LANES), BF16),
                        pltpu.VMEM((npairs, 16, LANES), F32)],
        compiler_params=_cparams(("parallel", "arbitrary")),
        name="moba_attn",
    )(qkv, qkv, qkv, tab, slopes)


def _attn_key_table(slope_vals, s):
    pos = jnp.arange(s)
    blk_of = pos // MOBA_BLOCK
    within = (pos % MOBA_BLOCK).astype(F32)
    tabs = []
    for h in range(N_ATTN_HEADS):
        ind = (blk_of[:, None] == jnp.arange(8)[None, :]).astype(F32)
        val = slope_vals[h] * within
        hi = val.astype(BF16).astype(F32)
        lo = val - hi
        spare = jnp.concatenate([ind, hi[:, None], lo[:, None], jnp.zeros((s, HEAD_DIM - 10), F32)], axis=1)
        own = jnp.zeros((s, HEAD_DIM), F32)
        tabs.append(jnp.concatenate([own, spare] if h % 2 == 0 else [spare, own], axis=1))
    return jnp.stack(tabs).astype(BF16)


def _ssd_kernel(xs_ref, bm_ref, cm_ref, dt_ref, sz_ref,
                dtb_ref, alog_ref, dskip_ref, nw_ref,
                y_ref,
                state_scr, xdec_scr, ydiag_scr):
    L = SSD_CHUNK
    hpg = N_SSM_HEADS // N_SSM_GROUPS

    xs = xs_ref[0]
    bm16 = bm_ref[0]
    cm16 = cm_ref[0]

    @pl.when(pl.program_id(2) == 0)
    def _():
        state_scr[...] = jnp.zeros_like(state_scr)

    xdt_in = dt_ref[0] + dtb_ref[...]
    dt = jnp.maximum(xdt_in, 0.0) + jnp.log1p(jnp.exp(-jnp.abs(xdt_in)))
    a2 = -jnp.exp(alog_ref[...]) * LOG2E
    da = dt * a2
    row = lax.broadcasted_iota(jnp.int32, (L, L), 0)
    col = lax.broadcasted_iota(jnp.int32, (L, L), 1)
    causal = row >= col
    tri = jnp.where(causal, 1.0, 0.0).astype(BF16)
    d_hi = da.astype(BF16)
    rem = da - d_hi.astype(F32)
    d_mid = rem.astype(BF16)
    d_lo = (rem - d_mid.astype(F32)).astype(BF16)
    csp = _dot(tri, jnp.concatenate([d_hi, d_mid, d_lo], axis=1))
    cs = (csp[:, 2 * LANES:] + csp[:, LANES:2 * LANES]) + csp[:, :LANES]
    cs_t = cs.T
    cs_last = cs[L - 1:L, :]

    cb = _dot_nt(cm16, bm16)
    lane = lax.broadcasted_iota(jnp.int32, (L, LANES), 1)
    lo_half = lane < SSM_HEAD_DIM

    cd_parts, eo_parts, xdt_parts = [], [], []
    for pr in range(hpg // 2):
        h0, h1 = 2 * pr, 2 * pr + 1
        xs_p = xs[:, pr * LANES:(pr + 1) * LANES]
        cs_p = jnp.where(lo_half, cs[:, h0:h0 + 1], cs[:, h1:h1 + 1])
        dt_p = jnp.where(lo_half, dt[:, h0:h0 + 1], dt[:, h1:h1 + 1])
        csl_p = jnp.where(lo_half[0:1], cs_last[:, h0:h0 + 1], cs_last[:, h1:h1 + 1])
        xdt_p = xs_p * dt_p
        xdt16 = xdt_p.astype(BF16)
        xdec_scr[:, pr * LANES:(pr + 1) * LANES] = (xdt_p * jnp.exp2(csl_p - cs_p)).astype(BF16)
        cd_parts.append(jnp.exp2(csl_p))
        eo_parts.append(jnp.exp2(cs_p))
        ys = []
        for h in (h0, h1):
            diff = cs[:, h:h + 1] - cs_t[h:h + 1, :]
            lmat = jnp.exp2(jnp.where(causal, diff, -jnp.inf))
            ys.append(_dot((cb * lmat).astype(BF16), xdt16))
        ydiag_scr[:, pr * LANES:(pr + 1) * LANES] = jnp.where(lo_half, ys[0], ys[1])

    eo = jnp.concatenate(eo_parts, axis=-1)
    cd = jnp.concatenate(cd_parts, axis=-1)
    state = state_scr[...]
    y = ydiag_scr[...] + eo * _dot(cm16, state.astype(BF16)) + xs * dskip_ref[...]
    state_scr[...] = state * cd + _dot(bm16.astype(F32).T.astype(BF16), xdec_scr[...])

    y = y * sz_ref[0]
    y = y * lax.rsqrt(jnp.mean(y * y, axis=-1, keepdims=True) + RMS_EPS) * nw_ref[...]
    y_ref[0] = y.astype(y_ref.dtype)


def _ssd(xs, bc, dt, sz, dtb, alog, dskip, norm_w):
    bsz, s, d_ssm = xs.shape
    L = SSD_CHUNK
    nc = s // L
    G = N_SSM_GROUPS
    gw = d_ssm // G

    def seq(width, off_fn):
        return pl.BlockSpec((1, L, width), lambda b, g, c: (b, c, off_fn(g)))

    def par(width):
        return pl.BlockSpec((1, width), lambda b, g, c: (0, g))

    in_specs = [
        seq(gw, lambda g: g), seq(D_STATE, lambda g: g), seq(D_STATE, lambda g: G + g),
        seq(LANES, lambda g: g), seq(gw, lambda g: g),
        par(LANES), par(LANES), par(gw), par(gw),
    ]
    return pl.pallas_call(
        _ssd_kernel,
        grid=(bsz, G, nc),
        in_specs=in_specs,
        out_specs=pl.BlockSpec((1, L, gw), lambda b, g, c: (b, c, g)),
        out_shape=jax.ShapeDtypeStruct((bsz, s, d_ssm), BF16),
        scratch_shapes=[pltpu.VMEM((D_STATE, gw), F32), pltpu.VMEM((L, gw), BF16), pltpu.VMEM((L, gw), F32)],
        compiler_params=_cparams(("parallel", "parallel", "arbitrary")),
        name="ssd",
    )(xs, bc, bc, dt, sz, dtb, alog, dskip, norm_w)


def _outproj_kernel(oa_ref, y_ref, x_ref, mod_ref, w_ref, g_ref, b_ref, o_ref, *, alpha, d_attn):
    acc = _dot(oa_ref[0], w_ref[0:d_attn, :]) + _dot(y_ref[0], w_ref[d_attn:, :])
    gate = mod_ref[0, 2:3, :]
    r = alpha * x_ref[0] + (1.0 + gate) * acc
    o_ref[0] = _layer_norm(r, g_ref[...], b_ref[...])


def _outproj(oa, y, x, mod3, w, g, b, alpha, tm=512):
    bsz, s, d = x.shape
    d_attn = oa.shape[-1]
    d_ssm = y.shape[-1]
    kern = functools.partial(_outproj_kernel, alpha=alpha, d_attn=d_attn)
    return pl.pallas_call(
        kern,
        grid=(bsz, s // tm),
        in_specs=[pl.BlockSpec((1, tm, d_attn), lambda b, i: (b, i, 0)),
                  pl.BlockSpec((1, tm, d_ssm), lambda b, i: (b, i, 0)),
                  pl.BlockSpec((1, tm, d), lambda b, i: (b, i, 0)),
                  pl.BlockSpec((1, 6, d), lambda b, i: (b, 0, 0)),
                  pl.BlockSpec((d_attn + d_ssm, d), lambda b, i: (0, 0), pipeline_mode=pl.Buffered(1)),
                  pl.BlockSpec((1, d), lambda b, i: (0, 0)),
                  pl.BlockSpec((1, d), lambda b, i: (0, 0))],
        out_specs=pl.BlockSpec((1, tm, d), lambda b, i: (b, i, 0)),
        out_shape=jax.ShapeDtypeStruct((bsz, s, d), F32),
        compiler_params=_cparams(("parallel", "parallel")),
        name="outproj_ln",
    )(oa, y, x, mod3, w, g.reshape(1, d), b.reshape(1, d))


def _ffn_kernel(x_ref, mod_ref, wu_ref, cw_ref, cb_ref, wd_ref, g_ref, b_ref, o_ref,
                h_scr, halo_scr, act_scr, *, alpha, d_ff, chunk):
    tm = x_ref.shape[1]
    taps = cw_ref.shape[0]
    first = pl.program_id(1) == 0

    @pl.when(first)
    def _():
        halo_scr[...] = jnp.zeros_like(halo_scr)

    x = x_ref[0]
    sh = mod_ref[0, 3:4, :]
    sc = mod_ref[0, 4:5, :]
    h_scr[...] = (x * (1.0 + sc) + sh).astype(BF16)

    def conv_cols(c0):
        u = _dot(h_scr[...], wu_ref[:, c0:c0 + chunk])
        ext = jnp.concatenate([halo_scr[:, c0:c0 + chunk], u], axis=0)
        halo_scr[:, c0:c0 + chunk] = u[tm - HALO:tm, :]
        return _causal_conv_rows(ext, cw_ref[:, c0:c0 + chunk]) + cb_ref[:, c0:c0 + chunk]

    for c0 in range(0, d_ff, chunk):
        gte = conv_cols(c0)
        val = conv_cols(d_ff + c0)
        act_scr[:, c0:c0 + chunk] = (gte * _sigmoid(gte) * val).astype(BF16)

    y = _dot(act_scr[...], wd_ref[...])
    gate = mod_ref[0, 5:6, :]
    r = alpha * x + (1.0 + gate) * y
    o_ref[0] = _layer_norm(r, g_ref[...], b_ref[...])


def _ffn(x, mod3, w_up, conv_w, conv_b, w_down, g, b, alpha, tm=256, chunk=256):
    bsz, s, d = x.shape
    d_ff = w_down.shape[0]
    taps = conv_w.shape[0]
    kern = functools.partial(_ffn_kernel, alpha=alpha, d_ff=d_ff, chunk=chunk)
    return pl.pallas_call(
        kern,
        grid=(bsz, s // tm),
        in_specs=[pl.BlockSpec((1, tm, d), lambda b, i: (b, i, 0)),
                  pl.BlockSpec((1, 6, d), lambda b, i: (b, 0, 0)),
                  pl.BlockSpec((d, 2 * d_ff), lambda b, i: (0, 0), pipeline_mode=pl.Buffered(1)),
                  pl.BlockSpec((taps, 2 * d_ff), lambda b, i: (0, 0)),
                  pl.BlockSpec((1, 2 * d_ff), lambda b, i: (0, 0)),
                  pl.BlockSpec((d_ff, d), lambda b, i: (0, 0), pipeline_mode=pl.Buffered(1)),
                  pl.BlockSpec((1, d), lambda b, i: (0, 0)),
                  pl.BlockSpec((1, d), lambda b, i: (0, 0))],
        out_specs=pl.BlockSpec((1, tm, d), lambda b, i: (b, i, 0)),
        out_shape=jax.ShapeDtypeStruct((bsz, s, d), F32),
        scratch_shapes=[pltpu.VMEM((tm, d), BF16),
                        pltpu.VMEM((HALO, 2 * d_ff), F32),
                        pltpu.VMEM((tm, d_ff), BF16)],
        compiler_params=_cparams(("parallel", "arbitrary")),
        name="ffn_ln",
    )(x, mod3, w_up, conv_w, conv_b.reshape(1, 2 * d_ff), w_down, g.reshape(1, d), b.reshape(1, d))


def _pad_cols(a, width):
    return jnp.pad(a, ((0, 0), (0, width - a.shape[1])))


def kernel(x, c, ada_w, ada_b, mix_in_w, ssm_conv_w, ssm_conv_b, ssm_dt_bias, ssm_a_log, ssm_d, ssm_norm_w,
           mix_out_w, ln1_g, ln1_b, ffn_up_w, ffn_conv_w, ffn_conv_b, ffn_down_w, ln2_g, ln2_b):
    depth = ada_w.shape[0]
    bsz, s, d = x.shape
    assert s % max(MOBA_BLOCK, SSD_CHUNK) == 0
    d_attn = N_ATTN_HEADS * HEAD_DIM
    d_ssm = N_SSM_HEADS * SSM_HEAD_DIM
    d_xbc = d_ssm + 2 * N_SSM_GROUPS * D_STATE
    hpg = N_SSM_HEADS // N_SSM_GROUPS
    alpha = (2.0 * depth) ** 0.25

    slope_vals = 2.0 ** (-8.0 * jnp.arange(1, N_ATTN_HEADS + 1, dtype=F32) / N_ATTN_HEADS)
    slopes = jnp.broadcast_to(slope_vals.reshape(N_ATTN_HEADS // 2, 2, 1), (N_ATTN_HEADS // 2, 2, MOBA_BLOCK))
    key_tab = _attn_key_table(slope_vals, s)

    def per_group_lanes(v):
        return _pad_cols(v.reshape(N_SSM_GROUPS, hpg), LANES).reshape(1, N_SSM_GROUPS * LANES)

    for li in range(depth):
        mod3 = _adaln(c, ada_w[li], ada_b[li]).reshape(bsz, 6, d)

        w_in = mix_in_w[li]
        n_main = 3 * d_attn + d_ssm + d_xbc
        w_dt = w_in[:, n_main:].reshape(d, N_SSM_GROUPS, hpg)
        w_dt = jnp.pad(w_dt, ((0, 0), (0, 0), (0, LANES - hpg))).reshape(d, N_SSM_GROUPS * LANES)
        w_pad = jnp.concatenate([w_in[:, :n_main], w_dt], axis=1).astype(BF16)
        segs = ((0, 3 * d_attn), (3 * d_attn, d_ssm), (3 * d_attn + d_ssm, d_xbc), (n_main, N_SSM_GROUPS * LANES))
        qkv, sz, xs, bc, dt = _inproj(x, mod3, w_pad, ssm_conv_w[li], ssm_conv_b[li].reshape(1, d_xbc), segs, d_ssm)

        o_attn = _attention(qkv, key_tab, slopes)
        y_ssm = _ssd(xs, bc, dt, sz, per_group_lanes(ssm_dt_bias[li]), per_group_lanes(ssm_a_log[li]),
                     jnp.repeat(ssm_d[li], SSM_HEAD_DIM).reshape(1, d_ssm), ssm_norm_w[li].reshape(1, d_ssm))

        x = _outproj(o_attn, y_ssm, x, mod3, mix_out_w[li].astype(BF16), ln1_g[li], ln1_b[li], alpha)
        x = _ffn(x, mod3, ffn_up_w[li].astype(BF16), ffn_conv_w[li], ffn_conv_b[li],
                 ffn_down_w[li].astype(BF16), ln2_g[li], ln2_b[li], alpha)
    return x
```

```python
import functools

import jax
import jax.numpy as jnp
import numpy as np
from jax import lax
from jax.experimental import pallas as pl
from jax.experimental.pallas import tpu as pltpu

F32 = jnp.float32
BF16 = jnp.bfloat16

N_ATTN_HEADS = 8
HEAD_DIM = 64
MOBA_BLOCK = 256
MOBA_TOPK = 3
N_SSM_HEADS = 16
SSM_HEAD_DIM = 64
N_SSM_GROUPS = 2
D_STATE = 128
SSD_CHUNK = 256
LN_EPS = 1e-5
RMS_EPS = 1e-5

LANES = 128
HALO = 8
LOG2E = 1.4426950408889634
NEG_BIG = -1e30
VMEM_LIMIT = 52 * 1024 * 1024


def _cparams(sem, flags=None):
    return pltpu.CompilerParams(dimension_semantics=sem, vmem_limit_bytes=VMEM_LIMIT, flags=flags)


def _sigmoid(x):
    return 1.0 / (1.0 + jnp.exp(-x))


def _dot(a, b):
    return jnp.dot(a, b, preferred_element_type=F32)


def _dot_nt(a, b):
    return lax.dot_general(a, b, (((1,), (1,)), ((), ())), preferred_element_type=F32)


def _causal_conv_rows(ext, w):
    taps = w.shape[0]
    acc = w[0:1, :] * ext
    for j in range(1, taps):
        acc = pltpu.roll(acc, 1, axis=0) + w[j:j + 1, :] * ext
    return acc[HALO:, :]


def _layer_norm(r, g, b):
    mu = jnp.mean(r, axis=-1, keepdims=True)
    d = r - mu
    var = jnp.mean(d * d, axis=-1, keepdims=True)
    return d * lax.rsqrt(var + LN_EPS) * g + b


def _adaln_kernel(c_ref, w_ref, b_ref, o_ref):
    c = c_ref[...]
    ca = (c * _sigmoid(c)).astype(BF16)
    o_ref[...] = _dot(ca, w_ref[...].astype(BF16)) + b_ref[...]


def _adaln(c, w, b):
    bsz, d = c.shape
    n = w.shape[1]
    tn = d
    return pl.pallas_call(
        _adaln_kernel,
        grid=(n // tn,),
        in_specs=[pl.BlockSpec((bsz, d), lambda j: (0, 0)),
                  pl.BlockSpec((d, tn), lambda j: (0, j)),
                  pl.BlockSpec((1, tn), lambda j: (0, j))],
        out_specs=pl.BlockSpec((bsz, tn), lambda j: (0, j)),
        out_shape=jax.ShapeDtypeStruct((bsz, n), F32),
        compiler_params=_cparams(("parallel",)),
        name="adaln",
    )(c, w, b.reshape(1, n))


def _inproj_kernel(x_ref, mod_ref, w_ref, wdt_ref, cw_ref, cb_ref, qkv_ref, sz_ref, xs_ref, bc_ref, dt_ref,
                   h_scr, halo_scr, *, segs, chunk):
    tm = x_ref.shape[1]

    @pl.when(pl.program_id(1) == 0)
    def _():
        halo_scr[...] = jnp.zeros_like(halo_scr)

    sh = mod_ref[0, 0:1, :]
    sc = mod_ref[0, 1:2, :]
    h_scr[...] = (x_ref[0] * (1.0 + sc) + sh).astype(BF16)

    def cols(c0, n):
        return _dot(h_scr[...], w_ref[:, c0:c0 + n])

    d_x = xs_ref.shape[-1]

    def conv_task(a):
        u = cols(segs[2][0] + a, chunk)
        ext = jnp.concatenate([halo_scr[:, a:a + chunk], u], axis=0)
        halo_scr[:, a:a + chunk] = u[tm - HALO:tm, :]
        act = _causal_conv_rows(ext, cw_ref[:, a:a + chunk]) + cb_ref[:, a:a + chunk]
        act = act * _sigmoid(act)
        if a < d_x:
            xs_ref[0, :, a:a + chunk] = act
        else:
            bc_ref[0, :, a - d_x:a - d_x + chunk] = act.astype(bc_ref.dtype)

    def gate_task(a):
        zc = cols(segs[1][0] + a, chunk)
        sz_ref[0, :, a:a + chunk] = zc * _sigmoid(zc)

    def qkv_task(a):
        res = cols(segs[0][0] + a, chunk).astype(qkv_ref.dtype)
        for p in range(chunk // LANES):
            qkv_ref[0, a // LANES + p] = res[:, p * LANES:(p + 1) * LANES]

    def dt_task(a):
        dt_ref[0] = _dot(h_scr[...], wdt_ref[...])

    heavy = [(conv_task, a) for a in range(0, segs[2][1], chunk)] + [(gate_task, a) for a in range(0, segs[1][1], chunk)]
    plain = [(qkv_task, a) for a in range(0, segs[0][1], chunk)] + [(dt_task, 0)]
    for k in range(max(len(heavy), len(plain))):
        for group in (heavy, plain):
            if k < len(group):
                fn, a = group[k]
                fn(a)


def _inproj(x, mod3, w_main, w_dt, conv_w, conv_b, segs, d_x, tm=512, chunk=512):
    bsz, s, d = x.shape
    ntot = w_main.shape[1]
    segs = tuple(segs) + ((0, w_dt.shape[1]),)
    widths = [w for _, w in segs]
    assert all(w % chunk == 0 for w in widths[:3]) and d_x % chunk == 0
    nslab = widths[0] // LANES
    d_bc = widths[2] - d_x
    taps = conv_w.shape[0]
    kern = functools.partial(_inproj_kernel, segs=segs, chunk=chunk)

    def rows(width):
        return pl.BlockSpec((1, tm, width), lambda b, i: (b, i, 0))

    return pl.pallas_call(
        kern,
        grid=(bsz, s // tm),
        in_specs=[rows(d),
                  pl.BlockSpec((1, 6, d), lambda b, i: (b, 0, 0)),
                  pl.BlockSpec((d, ntot), lambda b, i: (0, 0), pipeline_mode=pl.Buffered(1)),
                  pl.BlockSpec((d, widths[3]), lambda b, i: (0, 0)),
                  pl.BlockSpec((taps, widths[2]), lambda b, i: (0, 0)),
                  pl.BlockSpec((1, widths[2]), lambda b, i: (0, 0))],
        out_specs=[pl.BlockSpec((1, nslab, tm, LANES), lambda b, i: (b, 0, i, 0)),
                   rows(widths[1]), rows(d_x), rows(d_bc), rows(widths[3])],
        out_shape=[jax.ShapeDtypeStruct((bsz, nslab, s, LANES), BF16),
                   jax.ShapeDtypeStruct((bsz, s, widths[1]), F32),
                   jax.ShapeDtypeStruct((bsz, s, d_x), F32),
                   jax.ShapeDtypeStruct((bsz, s, d_bc), BF16),
                   jax.ShapeDtypeStruct((bsz, s, widths[3]), F32)],
        scratch_shapes=[pltpu.VMEM((tm, d), BF16), pltpu.VMEM((HALO, widths[2]), F32)],
        compiler_params=_cparams(("parallel", "arbitrary")),
        name="inproj",
    )(x, mod3, w_main, w_dt, conv_w, conv_b)


def _attn_kernel(q_ref, k_ref, v_ref, tab_ref, slope_ref, o_ref,
                 kt_scr, va_scr, kmean_scr, *, nb, npairs, scale):
    blk = MOBA_BLOCK
    i = pl.program_id(1)
    lane = lax.broadcasted_iota(jnp.int32, (blk, LANES), 1)

    @pl.when(i == 0)
    def _():
        kmean_scr[...] = jnp.zeros_like(kmean_scr)

        def prep(pr, carry):
            for j in range(nb):
                kj = k_ref[0, pr, j * blk:(j + 1) * blk, :]
                vj = v_ref[0, pr, j * blk:(j + 1) * blk, :]
                kmean_scr[pr, j:j + 1, :] = jnp.mean(kj.astype(F32), axis=0, keepdims=True)
                for hh in range(2):
                    own = (lane // HEAD_DIM) == hh
                    k_aug = jnp.where(own, kj, tab_ref[2 * pr + hh, j * blk:(j + 1) * blk, :])
                    kt_scr[2 * pr + hh, :, j * blk:(j + 1) * blk] = k_aug.astype(F32).T.astype(BF16)
                    va_scr[2 * pr + hh, j * blk:(j + 1) * blk, :] = jnp.where(own, vj, jnp.ones_like(vj))
            return carry

        lax.fori_loop(0, npairs, prep, 0)

    row = lax.broadcasted_iota(jnp.int32, (blk, blk), 0)
    col = lax.broadcasted_iota(jnp.int32, (blk, blk), 1)
    causal = row >= col
    blkid = lax.broadcasted_iota(jnp.int32, (8, blk), 0)
    past = blkid < i
    ones_t = jnp.where(blkid < 2, 1.0, 0.0).astype(F32)
    dist = ((i - blkid) * blk).astype(F32)

    q_augs = []
    for pr in range(npairs):
        q = q_ref[0, pr]
        q_scaled = q * jnp.asarray(scale, BF16)
        kmean = kmean_scr[pr].astype(BF16)
        for hh in range(2):
            own = (lane // HEAD_DIM) == hh
            qh = jnp.where(own, q, jnp.zeros_like(q))

            gate = _dot_nt(kmean, qh)[0:8, :]
            gate = jnp.where(past, gate, -jnp.inf)
            rank = jnp.zeros((8, blk), F32)
            for jp in range(nb):
                gj = gate[jp:jp + 1, :]
                beats = (gj > gate) | ((gj == gate) & (blkid > jp))
                rank = rank + jnp.where(beats, 1.0, 0.0)
            sel = (rank < MOBA_TOPK) & past
            slope = slope_ref[pr, hh:hh + 1, :]
            bias_t = jnp.where(sel, -slope * dist, NEG_BIG)
            bias_t = jnp.where(blkid == i, 0.0, bias_t)
            pieces = [bias_t, ones_t, jnp.zeros((HEAD_DIM - 16, blk), F32)]
            zeros_own = jnp.zeros((HEAD_DIM, blk), F32)
            pieces = [zeros_own] + pieces if hh == 0 else pieces + [zeros_own]
            extra = jnp.concatenate(pieces, axis=0).T
            q_augs.append(jnp.where(own, q_scaled, extra.astype(BF16)))

    def attend(n):
        for pr in range(npairs):
            outs = []
            for hh in range(2):
                h = 2 * pr + hh
                s = _dot(q_augs[h], kt_scr[h, :, 0:n * blk])
                parts = [s[:, j * blk:(j + 1) * blk] for j in range(n)]
                parts[-1] = jnp.where(causal, parts[-1], NEG_BIG)
                m = parts[-1]
                for part in parts[:-1]:
                    m = jnp.maximum(m, part)
                m = jnp.max(m, axis=-1, keepdims=True)
                p = jnp.concatenate([jnp.exp(part - m).astype(BF16) for part in parts], axis=1)
                acc = _dot(p, va_scr[h, 0:n * blk, :])
                outs.append(acc / pltpu.roll(acc, HEAD_DIM, axis=1))
            o_pair = jnp.where((lane // HEAD_DIM) == 0, outs[0], outs[1])
            o_ref[0, :, pr * LANES:(pr + 1) * LANES] = o_pair.astype(o_ref.dtype)

    for n in range(1, nb + 1):
        pl.when(i == n - 1)(functools.partial(attend, n))


def _attention(qkv, tab, slopes):
    bsz, nslab, s, _ = qkv.shape
    blk = MOBA_BLOCK
    nb = s // blk
    assert nb <= 8
    npairs = N_ATTN_HEADS // 2
    assert nslab == 3 * npairs
    kern = functools.partial(_attn_kernel, nb=nb, npairs=npairs, scale=HEAD_DIM ** -0.5)
    return pl.pallas_call(
        kern,
        grid=(bsz, nb),
        in_specs=[pl.BlockSpec((1, npairs, blk, LANES), lambda b, i: (b, 0, i, 0)),
                  pl.BlockSpec((1, npairs, s, LANES), lambda b, i: (b, 1, 0, 0)),
                  pl.BlockSpec((1, npairs, s, LANES), lambda b, i: (b, 2, 0, 0)),
                  pl.BlockSpec((N_ATTN_HEADS, s, LANES), lambda b, i: (0, 0, 0), pipeline_mode=pl.Buffered(1)),
                  pl.BlockSpec((npairs, 2, blk), lambda b, i: (0, 0, 0))],
        out_specs=pl.BlockSpec((1, blk, npairs * LANES), lambda b, i: (b, i, 0)),
        out_shape=jax.ShapeDtypeStruct((bsz, s, npairs * LANES), BF16),
        scratch_shapes=[pltpu.VMEM((N_ATTN_HEADS, LANES, s), BF16),
                        pltpu.VMEM((N_ATTN_HEADS, s, LANES), BF16),
                        pltpu.VMEM((npairs, 16, LANES), F32)],
        compiler_params=_cparams(("parallel", "arbitrary")),
        name="moba_attn",
    )(qkv, qkv, qkv, tab, slopes)


def _attn_key_table(slope_vals, s):
    pos = np.arange(s)
    blk_of = pos // MOBA_BLOCK
    within = (pos % MOBA_BLOCK).astype(np.float32)
    tabs = []
    for h in range(N_ATTN_HEADS):
        ind = (blk_of[:, None] == np.arange(8)[None, :]).astype(np.float32)
        val = (slope_vals[h] * within).astype(np.float32)
        hi = val.astype(BF16).astype(np.float32)
        lo = val - hi
        spare = np.concatenate([ind, hi[:, None], lo[:, None], np.zeros((s, HEAD_DIM - 10), np.float32)], axis=1)
        own = np.zeros((s, HEAD_DIM), np.float32)
        tabs.append(np.concatenate([own, spare] if h % 2 == 0 else [spare, own], axis=1))
    return jnp.asarray(np.stack(tabs).astype(BF16))


def _ssd_kernel(xs_ref, bc_ref, dt_ref, sz_ref, dtb_ref, alog_ref, dskip_ref, nw_ref, y_ref,
                state_scr, xdec_scr, ydiag_scr):
    L = SSD_CHUNK

    @pl.when(pl.program_id(1) == 0)
    def _():
        state_scr[...] = jnp.zeros_like(state_scr)

    row = lax.broadcasted_iota(jnp.int32, (L, L), 0)
    col = lax.broadcasted_iota(jnp.int32, (L, L), 1)
    causal = row >= col
    tri = jnp.where(causal, 1.0, 0.0).astype(BF16)
    for g in range(N_SSM_GROUPS):
        _ssd_group(g, causal, tri, xs_ref, bc_ref, dt_ref, sz_ref, dtb_ref, alog_ref, dskip_ref, nw_ref, y_ref,
                   state_scr, xdec_scr, ydiag_scr)


def _ssd_group(g, causal, tri, xs_ref, bc_ref, dt_ref, sz_ref, dtb_ref, alog_ref, dskip_ref, nw_ref, y_ref,
               state_scr, xdec_scr, ydiag_scr):
    L = SSD_CHUNK
    hpg = N_SSM_HEADS // N_SSM_GROUPS
    gw = hpg * SSM_HEAD_DIM
    gcols = slice(g * gw, (g + 1) * gw)
    glanes = slice(g * LANES, (g + 1) * LANES)
    xs = xs_ref[0, :, gcols]
    bm16 = bc_ref[0, :, g * D_STATE:(g + 1) * D_STATE]
    cm16 = bc_ref[0, :, (N_SSM_GROUPS + g) * D_STATE:(N_SSM_GROUPS + g + 1) * D_STATE]

    xdt_in = dt_ref[0, :, glanes] + dtb_ref[:, glanes]
    dt = jnp.maximum(xdt_in, 0.0) + jnp.log1p(jnp.exp(-jnp.abs(xdt_in)))
    a2 = -jnp.exp(alog_ref[:, glanes]) * LOG2E
    da = dt * a2
    d_hi = da.astype(BF16)
    rem = da - d_hi.astype(F32)
    d_mid = rem.astype(BF16)
    d_lo = (rem - d_mid.astype(F32)).astype(BF16)
    csp = _dot(tri, jnp.concatenate([d_hi, d_mid, d_lo], axis=1))
    cs = (csp[:, 2 * LANES:] + csp[:, LANES:2 * LANES]) + csp[:, :LANES]
    cs_t = cs.T
    cs_last = cs[L - 1:L, :]

    cb = _dot_nt(cm16, bm16)
    lane = lax.broadcasted_iota(jnp.int32, (L, LANES), 1)
    lo_half = lane < SSM_HEAD_DIM

    cd_parts, eo_parts, xdt_parts = [], [], []
    for pr in range(hpg // 2):
        h0, h1 = 2 * pr, 2 * pr + 1
        xs_p = xs[:, pr * LANES:(pr + 1) * LANES]
        cs_p = jnp.where(lo_half, cs[:, h0:h0 + 1], cs[:, h1:h1 + 1])
        dt_p = jnp.where(lo_half, dt[:, h0:h0 + 1], dt[:, h1:h1 + 1])
        csl_p = jnp.where(lo_half[0:1], cs_last[:, h0:h0 + 1], cs_last[:, h1:h1 + 1])
        xdt_p = xs_p * dt_p
        xdt16 = xdt_p.astype(BF16)
        xdec_scr[g, :, pr * LANES:(pr + 1) * LANES] = (xdt_p * jnp.exp2(csl_p - cs_p)).astype(BF16)
        cd_parts.append(jnp.exp2(csl_p))
        eo_parts.append(jnp.exp2(cs_p))
        ys = []
        for h in (h0, h1):
            diff = cs[:, h:h + 1] - cs_t[h:h + 1, :]
            lmat = jnp.exp2(jnp.where(causal, diff, -jnp.inf))
            ys.append(_dot((cb * lmat).astype(BF16), xdt16))
        ydiag_scr[g, :, pr * LANES:(pr + 1) * LANES] = jnp.where(lo_half, ys[0], ys[1])

    eo = jnp.concatenate(eo_parts, axis=-1)
    cd = jnp.concatenate(cd_parts, axis=-1)
    state = state_scr[g]
    y = ydiag_scr[g] + eo * _dot(cm16, state.astype(BF16)) + xs * dskip_ref[:, gcols]
    state_scr[g] = state * cd + _dot(bm16.astype(F32).T.astype(BF16), xdec_scr[g])

    y = y * sz_ref[0, :, gcols]
    y = y * lax.rsqrt(jnp.mean(y * y, axis=-1, keepdims=True) + RMS_EPS) * nw_ref[:, gcols]
    y_ref[0, :, gcols] = y.astype(y_ref.dtype)


def _ssd(xs, bc, dt, sz, dtb, alog, dskip, norm_w):
    bsz, s, d_ssm = xs.shape
    L = SSD_CHUNK
    nc = s // L
    G = N_SSM_GROUPS
    gw = d_ssm // G

    def seq(width):
        return pl.BlockSpec((1, L, width), lambda b, c: (b, c, 0))

    def par(width):
        return pl.BlockSpec((1, width), lambda b, c: (0, 0))

    in_specs = [seq(d_ssm), seq(bc.shape[-1]), seq(G * LANES), seq(d_ssm),
                par(G * LANES), par(G * LANES), par(d_ssm), par(d_ssm)]
    return pl.pallas_call(
        _ssd_kernel,
        grid=(bsz, nc),
        in_specs=in_specs,
        out_specs=seq(d_ssm),
        out_shape=jax.ShapeDtypeStruct((bsz, s, d_ssm), BF16),
        scratch_shapes=[pltpu.VMEM((G, D_STATE, gw), F32), pltpu.VMEM((G, L, gw), BF16),
                        pltpu.VMEM((G, L, gw), F32)],
        compiler_params=_cparams(("parallel", "arbitrary")),
        name="ssd",
    )(xs, bc, dt, sz, dtb, alog, dskip, norm_w)


def _outproj_kernel(oa_ref, y_ref, x_ref, mod_ref, w_ref, g_ref, b_ref, o_ref, *, alpha, d_attn):
    acc = _dot(oa_ref[0], w_ref[0:d_attn, :]) + _dot(y_ref[0], w_ref[d_attn:, :])
    gate = mod_ref[0, 2:3, :]
    r = alpha * x_ref[0] + (1.0 + gate) * acc
    o_ref[0] = _layer_norm(r, g_ref[...], b_ref[...])


def _outproj(oa, y, x, mod3, w, g, b, alpha, tm=512):
    bsz, s, d = x.shape
    d_attn = oa.shape[-1]
    d_ssm = y.shape[-1]
    kern = functools.partial(_outproj_kernel, alpha=alpha, d_attn=d_attn)
    return pl.pallas_call(
        kern,
        grid=(bsz, s // tm),
        in_specs=[pl.BlockSpec((1, tm, d_attn), lambda b, i: (b, i, 0)),
                  pl.BlockSpec((1, tm, d_ssm), lambda b, i: (b, i, 0)),
                  pl.BlockSpec((1, tm, d), lambda b, i: (b, i, 0)),
                  pl.BlockSpec((1, 6, d), lambda b, i: (b, 0, 0)),
                  pl.BlockSpec((d_attn + d_ssm, d), lambda b, i: (0, 0), pipeline_mode=pl.Buffered(1)),
                  pl.BlockSpec((1, d), lambda b, i: (0, 0)),
                  pl.BlockSpec((1, d), lambda b, i: (0, 0))],
        out_specs=pl.BlockSpec((1, tm, d), lambda b, i: (b, i, 0)),
        out_shape=jax.ShapeDtypeStruct((bsz, s, d), F32),
        compiler_params=_cparams(("parallel", "parallel")),
        name="outproj_ln",
    )(oa, y, x, mod3, w, g.reshape(1, d), b.reshape(1, d))


def _ffn_kernel(x_ref, mod_ref, wu_ref, cw_ref, cb_ref, wd_ref, g_ref, b_ref, o_ref,
                h_scr, halo_scr, act_scr, *, alpha, d_ff, chunk):
    tm = x_ref.shape[1]
    taps = cw_ref.shape[0]
    first = pl.program_id(1) == 0

    @pl.when(first)
    def _():
        halo_scr[...] = jnp.zeros_like(halo_scr)

    x = x_ref[0]
    sh = mod_ref[0, 3:4, :]
    sc = mod_ref[0, 4:5, :]
    h_scr[...] = (x * (1.0 + sc) + sh).astype(BF16)

    def conv_cols(c0):
        u = _dot(h_scr[...], wu_ref[:, c0:c0 + chunk])
        ext = jnp.concatenate([halo_scr[:, c0:c0 + chunk], u], axis=0)
        halo_scr[:, c0:c0 + chunk] = u[tm - HALO:tm, :]
        return _causal_conv_rows(ext, cw_ref[:, c0:c0 + chunk]) + cb_ref[:, c0:c0 + chunk]

    for c0 in range(0, d_ff, chunk):
        gte = conv_cols(c0)
        val = conv_cols(d_ff + c0)
        act_scr[:, c0:c0 + chunk] = (gte * _sigmoid(gte) * val).astype(BF16)

    y = _dot(act_scr[...], wd_ref[...])
    gate = mod_ref[0, 5:6, :]
    r = alpha * x + (1.0 + gate) * y
    o_ref[0] = _layer_norm(r, g_ref[...], b_ref[...])


def _ffn(x, mod3, w_up, conv_w, conv_b, w_down, g, b, alpha, tm=512, chunk=256):
    bsz, s, d = x.shape
    d_ff = w_down.shape[0]
    taps = conv_w.shape[0]
    kern = functools.partial(_ffn_kernel, alpha=alpha, d_ff=d_ff, chunk=chunk)
    return pl.pallas_call(
        kern,
        grid=(bsz, s // tm),
        in_specs=[pl.BlockSpec((1, tm, d), lambda b, i: (b, i, 0)),
                  pl.BlockSpec((1, 6, d), lambda b, i: (b, 0, 0)),
                  pl.BlockSpec((d, 2 * d_ff), lambda b, i: (0, 0), pipeline_mode=pl.Buffered(1)),
                  pl.BlockSpec((taps, 2 * d_ff), lambda b, i: (0, 0)),
                  pl.BlockSpec((1, 2 * d_ff), lambda b, i: (0, 0)),
                  pl.BlockSpec((d_ff, d), lambda b, i: (0, 0), pipeline_mode=pl.Buffered(1)),
                  pl.BlockSpec((1, d), lambda b, i: (0, 0)),
                  pl.BlockSpec((1, d), lambda b, i: (0, 0))],
        out_specs=pl.BlockSpec((1, tm, d), lambda b, i: (b, i, 0)),
        out_shape=jax.ShapeDtypeStruct((bsz, s, d), F32),
        scratch_shapes=[pltpu.VMEM((tm, d), BF16),
                        pltpu.VMEM((HALO, 2 * d_ff), F32),
                        pltpu.VMEM((tm, d_ff), BF16)],
        compiler_params=_cparams(("parallel", "arbitrary")),
        name="ffn_ln",
    )(x, mod3, w_up, conv_w, conv_b.reshape(1, 2 * d_ff), w_down, g.reshape(1, d), b.reshape(1, d))


def _pad_cols(a, width):
    return jnp.pad(a, ((0, 0), (0, width - a.shape[1])))


def kernel(x, c, ada_w, ada_b, mix_in_w, ssm_conv_w, ssm_conv_b, ssm_dt_bias, ssm_a_log, ssm_d, ssm_norm_w,
           mix_out_w, ln1_g, ln1_b, ffn_up_w, ffn_conv_w, ffn_conv_b, ffn_down_w, ln2_g, ln2_b):
    depth = ada_w.shape[0]
    bsz, s, d = x.shape
    assert s % max(MOBA_BLOCK, SSD_CHUNK) == 0
    d_attn = N_ATTN_HEADS * HEAD_DIM
    d_ssm = N_SSM_HEADS * SSM_HEAD_DIM
    d_xbc = d_ssm + 2 * N_SSM_GROUPS * D_STATE
    hpg = N_SSM_HEADS // N_SSM_GROUPS
    alpha = (2.0 * depth) ** 0.25

    slope_vals = (2.0 ** (-8.0 * np.arange(1, N_ATTN_HEADS + 1) / N_ATTN_HEADS)).astype(np.float32)
    slopes = jnp.asarray(np.broadcast_to(slope_vals.reshape(N_ATTN_HEADS // 2, 2, 1),
                                         (N_ATTN_HEADS // 2, 2, MOBA_BLOCK)))
    key_tab = _attn_key_table(slope_vals, s)

    def per_group_lanes(v):
        return _pad_cols(v.reshape(N_SSM_GROUPS, hpg), LANES).reshape(1, N_SSM_GROUPS * LANES)

    for li in range(depth):
        mod3 = _adaln(c, ada_w[li], ada_b[li]).reshape(bsz, 6, d)

        w_in = mix_in_w[li]
        n_main = 3 * d_attn + d_ssm + d_xbc
        w_dt = w_in[:, n_main:].reshape(d, N_SSM_GROUPS, hpg)
        w_dt = jnp.pad(w_dt, ((0, 0), (0, 0), (0, LANES - hpg))).reshape(d, N_SSM_GROUPS * LANES).astype(BF16)
        w_main = w_in[:, :n_main].astype(BF16)
        segs = ((0, 3 * d_attn), (3 * d_attn, d_ssm), (3 * d_attn + d_ssm, d_xbc))
        qkv, sz, xs, bc, dt = _inproj(x, mod3, w_main, w_dt, ssm_conv_w[li], ssm_conv_b[li].reshape(1, d_xbc),
                                      segs, d_ssm)

        o_attn = _attention(qkv, key_tab, slopes)
        y_ssm = _ssd(xs, bc, dt, sz, per_group_lanes(ssm_dt_bias[li]), per_group_lanes(ssm_a_log[li]),
                     jnp.repeat(ssm_d[li], SSM_HEAD_DIM).reshape(1, d_ssm), ssm_norm_w[li].reshape(1, d_ssm))

        x = _outproj(o_attn, y_ssm, x, mod3, mix_out_w[li].astype(BF16), ln1_g[li], ln1_b[li], alpha)
        x = _ffn(x, mod3, ffn_up_w[li].astype(BF16), ffn_conv_w[li], ffn_conv_b[li],
                 ffn_down_w[li].astype(BF16), ln2_g[li], ln2_b[li], alpha)
    return x
```

```python
import functools

import jax
import jax.numpy as jnp
import numpy as np
from jax import lax
from jax.experimental import pallas as pl
from jax.experimental.pallas import tpu as pltpu

F32 = jnp.float32
BF16 = jnp.bfloat16

N_ATTN_HEADS = 8
HEAD_DIM = 64
MOBA_BLOCK = 256
MOBA_TOPK = 3
N_SSM_HEADS = 16
SSM_HEAD_DIM = 64
N_SSM_GROUPS = 2
D_STATE = 128
SSD_CHUNK = 256
LN_EPS = 1e-5
RMS_EPS = 1e-5

LANES = 128
HALO = 8
ATTN_Q_BLOCKS = 2
LOG2E = 1.4426950408889634
NEG_BIG = -1e30
VMEM_LIMIT = 52 * 1024 * 1024


def _cparams(sem, flags=None):
    return pltpu.CompilerParams(dimension_semantics=sem, vmem_limit_bytes=VMEM_LIMIT, flags=flags)


def _sigmoid(x):
    return 1.0 / (1.0 + jnp.exp(-x))


def _dot(a, b):
    return jnp.dot(a, b, preferred_element_type=F32)


def _dot_nt(a, b):
    return lax.dot_general(a, b, (((1,), (1,)), ((), ())), preferred_element_type=F32)


def _causal_conv_rows(ext, w):
    taps = w.shape[0]
    acc = w[0:1, :] * ext
    for j in range(1, taps):
        acc = pltpu.roll(acc, 1, axis=0) + w[j:j + 1, :] * ext
    return acc[HALO:, :]


def _layer_norm(r, g, b):
    mu = jnp.mean(r, axis=-1, keepdims=True)
    d = r - mu
    var = jnp.mean(d * d, axis=-1, keepdims=True)
    return d * lax.rsqrt(var + LN_EPS) * g + b


def _adaln_kernel(c_ref, w_ref, b_ref, o_ref):
    c = c_ref[...]
    ca = (c * _sigmoid(c)).astype(BF16)
    o_ref[...] = _dot(ca, w_ref[...].astype(BF16)) + b_ref[...]


def _adaln(c, w, b):
    bsz, d = c.shape
    n = w.shape[1]
    tn = d
    return pl.pallas_call(
        _adaln_kernel,
        grid=(n // tn,),
        in_specs=[pl.BlockSpec((bsz, d), lambda j: (0, 0)),
                  pl.BlockSpec((d, tn), lambda j: (0, j)),
                  pl.BlockSpec((1, tn), lambda j: (0, j))],
        out_specs=pl.BlockSpec((bsz, tn), lambda j: (0, j)),
        out_shape=jax.ShapeDtypeStruct((bsz, n), F32),
        compiler_params=_cparams(("parallel",)),
        name="adaln",
    )(c, w, b.reshape(1, n))


def _inproj_kernel(x_ref, mod_ref, w_ref, wdt_ref, cw_ref, cb_ref, qkv_ref, sz_ref, xs_ref, bc_ref, dt_ref,
                   h_scr, halo_scr, *, segs, chunk):
    tm = x_ref.shape[1]

    @pl.when(pl.program_id(1) == 0)
    def _():
        halo_scr[...] = jnp.zeros_like(halo_scr)

    sh = mod_ref[0, 0:1, :]
    sc = mod_ref[0, 1:2, :]
    h_scr[...] = (x_ref[0] * (1.0 + sc) + sh).astype(BF16)

    def cols(c0, n):
        return _dot_nt(h_scr[...], w_ref[c0:c0 + n, :])

    d_x = xs_ref.shape[-1]

    def conv_task(a):
        u = cols(segs[2][0] + a, chunk)
        ext = jnp.concatenate([halo_scr[:, a:a + chunk], u], axis=0)
        halo_scr[:, a:a + chunk] = u[tm - HALO:tm, :]
        act = _causal_conv_rows(ext, cw_ref[:, a:a + chunk]) + cb_ref[:, a:a + chunk]
        act = act * _sigmoid(act)
        if a < d_x:
            xs_ref[0, :, a:a + chunk] = act
        else:
            bc_ref[0, :, a - d_x:a - d_x + chunk] = act.astype(bc_ref.dtype)

    def gate_task(a):
        zc = cols(segs[1][0] + a, chunk)
        sz_ref[0, :, a:a + chunk] = zc * _sigmoid(zc)

    def qkv_task(a):
        res = cols(segs[0][0] + a, chunk).astype(qkv_ref.dtype)
        for p in range(chunk // LANES):
            qkv_ref[0, a // LANES + p] = res[:, p * LANES:(p + 1) * LANES]

    def dt_task(a):
        dt_ref[0] = _dot(h_scr[...], wdt_ref[...])

    heavy = [(conv_task, a) for a in range(0, segs[2][1], chunk)] + [(gate_task, a) for a in range(0, segs[1][1], chunk)]
    plain = [(qkv_task, a) for a in range(0, segs[0][1], chunk)] + [(dt_task, 0)]
    for k in range(max(len(heavy), len(plain))):
        for group in (heavy, plain):
            if k < len(group):
                fn, a = group[k]
                fn(a)


def _inproj(x, mod3, w_t, w_dt, conv_w, conv_b, segs, d_x, tm=512, chunk=512):
    bsz, s, d = x.shape
    ntot = max(c0 + w for c0, w in segs)
    segs = tuple(segs) + ((0, w_dt.shape[1]),)
    widths = [w for _, w in segs]
    assert all(w % chunk == 0 for w in widths[:3]) and d_x % chunk == 0
    nslab = widths[0] // LANES
    d_bc = widths[2] - d_x
    taps = conv_w.shape[0]
    kern = functools.partial(_inproj_kernel, segs=segs, chunk=chunk)

    def rows(width):
        return pl.BlockSpec((1, tm, width), lambda b, i: (b, i, 0))

    return pl.pallas_call(
        kern,
        grid=(bsz, s // tm),
        in_specs=[rows(d),
                  pl.BlockSpec((1, 6, d), lambda b, i: (b, 0, 0)),
                  pl.BlockSpec((ntot, d), lambda b, i: (0, 0), pipeline_mode=pl.Buffered(1)),
                  pl.BlockSpec((d, widths[3]), lambda b, i: (0, 0)),
                  pl.BlockSpec((taps, widths[2]), lambda b, i: (0, 0)),
                  pl.BlockSpec((1, widths[2]), lambda b, i: (0, 0))],
        out_specs=[pl.BlockSpec((1, nslab, tm, LANES), lambda b, i: (b, 0, i, 0)),
                   rows(widths[1]), rows(d_x), rows(d_bc), rows(widths[3])],
        out_shape=[jax.ShapeDtypeStruct((bsz, nslab, s, LANES), BF16),
                   jax.ShapeDtypeStruct((bsz, s, widths[1]), F32),
                   jax.ShapeDtypeStruct((bsz, s, d_x), F32),
                   jax.ShapeDtypeStruct((bsz, s, d_bc), BF16),
                   jax.ShapeDtypeStruct((bsz, s, widths[3]), F32)],
        scratch_shapes=[pltpu.VMEM((tm, d), BF16), pltpu.VMEM((HALO, widths[2]), F32)],
        compiler_params=_cparams(("parallel", "arbitrary")),
        name="inproj",
    )(x, mod3, w_t, w_dt, conv_w, conv_b)


def _attn_kernel(q_ref, k_ref, v_ref, tab_ref, slope_ref, o_ref,
                 kt_scr, va_scr, kmean_scr, *, nb, npairs, scale, qpb):
    blk = MOBA_BLOCK
    i = pl.program_id(1)
    lane = lax.broadcasted_iota(jnp.int32, (blk, LANES), 1)

    @pl.when(i == 0)
    def _():
        kmean_scr[...] = jnp.zeros_like(kmean_scr)

        def prep(pr, carry):
            for j in range(nb):
                kj = k_ref[0, pr, j * blk:(j + 1) * blk, :]
                vj = v_ref[0, pr, j * blk:(j + 1) * blk, :]
                kmean_scr[pr, j:j + 1, :] = jnp.mean(kj.astype(F32), axis=0, keepdims=True)
                for hh in range(2):
                    own = (lane // HEAD_DIM) == hh
                    k_aug = jnp.where(own, kj, tab_ref[2 * pr + hh, j * blk:(j + 1) * blk, :])
                    kt_scr[2 * pr + hh, :, j * blk:(j + 1) * blk] = k_aug.astype(F32).T.astype(BF16)
                    va_scr[2 * pr + hh, j * blk:(j + 1) * blk, :] = jnp.where(own, vj, jnp.ones_like(vj))
            return carry

        lax.fori_loop(0, npairs, prep, 0)

    qrows = qpb * blk
    row = lax.broadcasted_iota(jnp.int32, (blk, blk), 0)
    col = lax.broadcasted_iota(jnp.int32, (blk, blk), 1)
    causal = row >= col
    lane_q = lax.broadcasted_iota(jnp.int32, (qrows, LANES), 1)
    blkid = lax.broadcasted_iota(jnp.int32, (8, qrows), 0)
    qblk = i * qpb + lax.broadcasted_iota(jnp.int32, (8, qrows), 1) // blk
    past = blkid < qblk
    ones_t = jnp.where(blkid < 2, 1.0, 0.0).astype(F32)
    dist = ((qblk - blkid) * blk).astype(F32)

    q_augs = []
    for pr in range(npairs):
        q = q_ref[0, pr]
        q_scaled = q * jnp.asarray(scale, BF16)
        kmean = kmean_scr[pr].astype(BF16)
        for hh in range(2):
            own = (lane_q // HEAD_DIM) == hh
            qh = jnp.where(own, q, jnp.zeros_like(q))

            gate = _dot_nt(kmean, qh)[0:8, :]
            gate = jnp.where(past, gate, -jnp.inf)
            rank = jnp.zeros((8, qrows), F32)
            for jp in range(nb):
                gj = gate[jp:jp + 1, :]
                beats = (gj > gate) | ((gj == gate) & (blkid > jp))
                rank = rank + jnp.where(beats, 1.0, 0.0)
            sel = (rank < MOBA_TOPK) & past
            slope = slope_ref[pr, hh:hh + 1, :]
            bias_t = jnp.where(sel, -slope * dist, NEG_BIG)
            bias_t = jnp.where(blkid == qblk, 0.0, bias_t)
            pieces = [bias_t, ones_t, jnp.zeros((HEAD_DIM - 16, qrows), F32)]
            zeros_own = jnp.zeros((HEAD_DIM, qrows), F32)
            pieces = [zeros_own] + pieces if hh == 0 else pieces + [zeros_own]
            extra = jnp.concatenate(pieces, axis=0).T
            q_augs.append(jnp.where(own, q_scaled, extra.astype(BF16)))

    def attend(step):
        for pr in range(npairs):
            for sub in range(qpb):
                n = step * qpb + sub + 1
                rows = slice(sub * blk, (sub + 1) * blk)
                outs = []
                for hh in range(2):
                    h = 2 * pr + hh
                    s = _dot(q_augs[h][rows], kt_scr[h, :, 0:n * blk])
                    parts = [s[:, j * blk:(j + 1) * blk] for j in range(n)]
                    parts[-1] = jnp.where(causal, parts[-1], NEG_BIG)
                    m = parts[-1]
                    for part in parts[:-1]:
                        m = jnp.maximum(m, part)
                    m = jnp.max(m, axis=-1, keepdims=True)
                    p = jnp.concatenate([jnp.exp(part - m).astype(BF16) for part in parts], axis=1)
                    acc = _dot(p, va_scr[h, 0:n * blk, :])
                    outs.append(acc / pltpu.roll(acc, HEAD_DIM, axis=1))
                o_pair = jnp.where((lane // HEAD_DIM) == 0, outs[0], outs[1])
                o_ref[0, rows, pr * LANES:(pr + 1) * LANES] = o_pair.astype(o_ref.dtype)

    for step in range(nb // qpb):
        pl.when(i == step)(functools.partial(attend, step))


def _attention(qkv, tab, slopes, qpb):
    bsz, nslab, s, _ = qkv.shape
    blk = MOBA_BLOCK
    nb = s // blk
    assert nb <= 8 and nb % qpb == 0
    npairs = N_ATTN_HEADS // 2
    assert nslab == 3 * npairs
    qrows = qpb * blk
    kern = functools.partial(_attn_kernel, nb=nb, npairs=npairs, scale=HEAD_DIM ** -0.5, qpb=qpb)
    return pl.pallas_call(
        kern,
        grid=(bsz, nb // qpb),
        in_specs=[pl.BlockSpec((1, npairs, qrows, LANES), lambda b, i: (b, 0, i, 0)),
                  pl.BlockSpec((1, npairs, s, LANES), lambda b, i: (b, 1, 0, 0)),
                  pl.BlockSpec((1, npairs, s, LANES), lambda b, i: (b, 2, 0, 0)),
                  pl.BlockSpec((N_ATTN_HEADS, s, LANES), lambda b, i: (0, 0, 0), pipeline_mode=pl.Buffered(1)),
                  pl.BlockSpec((npairs, 2, qrows), lambda b, i: (0, 0, 0))],
        out_specs=pl.BlockSpec((1, qrows, npairs * LANES), lambda b, i: (b, i, 0)),
        out_shape=jax.ShapeDtypeStruct((bsz, s, npairs * LANES), BF16),
        scratch_shapes=[pltpu.VMEM((N_ATTN_HEADS, LANES, s), BF16),
                        pltpu.VMEM((N_ATTN_HEADS, s, LANES), BF16),
                        pltpu.VMEM((npairs, 16, LANES), F32)],
        compiler_params=_cparams(("parallel", "arbitrary")),
        name="moba_attn",
    )(qkv, qkv, qkv, tab, slopes)


def _attn_key_table(slope_vals, s):
    pos = np.arange(s)
    blk_of = pos // MOBA_BLOCK
    within = (pos % MOBA_BLOCK).astype(np.float32)
    tabs = []
    for h in range(N_ATTN_HEADS):
        ind = (blk_of[:, None] == np.arange(8)[None, :]).astype(np.float32)
        val = (slope_vals[h] * within).astype(np.float32)
        hi = val.astype(BF16).astype(np.float32)
        lo = val - hi
        spare = np.concatenate([ind, hi[:, None], lo[:, None], np.zeros((s, HEAD_DIM - 10), np.float32)], axis=1)
        own = np.zeros((s, HEAD_DIM), np.float32)
        tabs.append(np.concatenate([own, spare] if h % 2 == 0 else [spare, own], axis=1))
    return jnp.asarray(np.stack(tabs).astype(BF16))


def _ssd_kernel(xs_ref, bc_ref, dt_ref, sz_ref, dtb_ref, alog_ref, dskip_ref, nw_ref, y_ref,
                state_scr, xdec_scr, ydiag_scr):
    L = SSD_CHUNK

    @pl.when(pl.program_id(1) == 0)
    def _():
        state_scr[...] = jnp.zeros_like(state_scr)

    row = lax.broadcasted_iota(jnp.int32, (L, L), 0)
    col = lax.broadcasted_iota(jnp.int32, (L, L), 1)
    causal = row >= col
    tri = jnp.where(causal, 1.0, 0.0).astype(BF16)
    for g in range(N_SSM_GROUPS):
        _ssd_group(g, causal, tri, xs_ref, bc_ref, dt_ref, sz_ref, dtb_ref, alog_ref, dskip_ref, nw_ref, y_ref,
                   state_scr, xdec_scr, ydiag_scr)


def _ssd_group(g, causal, tri, xs_ref, bc_ref, dt_ref, sz_ref, dtb_ref, alog_ref, dskip_ref, nw_ref, y_ref,
               state_scr, xdec_scr, ydiag_scr):
    L = SSD_CHUNK
    hpg = N_SSM_HEADS // N_SSM_GROUPS
    gw = hpg * SSM_HEAD_DIM
    gcols = slice(g * gw, (g + 1) * gw)
    glanes = slice(g * LANES, (g + 1) * LANES)
    xs = xs_ref[0, :, gcols]
    bm16 = bc_ref[0, :, g * D_STATE:(g + 1) * D_STATE]
    cm16 = bc_ref[0, :, (N_SSM_GROUPS + g) * D_STATE:(N_SSM_GROUPS + g + 1) * D_STATE]

    xdt_in = dt_ref[0, :, glanes] + dtb_ref[:, glanes]
    dt = jnp.maximum(xdt_in, 0.0) + jnp.log1p(jnp.exp(-jnp.abs(xdt_in)))
    a2 = -jnp.exp(alog_ref[:, glanes]) * LOG2E
    da = dt * a2
    d_hi = da.astype(BF16)
    rem = da - d_hi.astype(F32)
    d_mid = rem.astype(BF16)
    d_lo = (rem - d_mid.astype(F32)).astype(BF16)
    csp = _dot(tri, jnp.concatenate([d_hi, d_mid, d_lo], axis=1))
    cs = (csp[:, 2 * LANES:] + csp[:, LANES:2 * LANES]) + csp[:, :LANES]
    cs_t = cs.T
    cs_last = cs[L - 1:L, :]

    cb = _dot_nt(cm16, bm16)
    lane = lax.broadcasted_iota(jnp.int32, (L, LANES), 1)
    lo_half = lane < SSM_HEAD_DIM

    cd_parts, eo_parts, xdt_parts = [], [], []
    for pr in range(hpg // 2):
        h0, h1 = 2 * pr, 2 * pr + 1
        xs_p = xs[:, pr * LANES:(pr + 1) * LANES]
        cs_p = jnp.where(lo_half, cs[:, h0:h0 + 1], cs[:, h1:h1 + 1])
        dt_p = jnp.where(lo_half, dt[:, h0:h0 + 1], dt[:, h1:h1 + 1])
        csl_p = jnp.where(lo_half[0:1], cs_last[:, h0:h0 + 1], cs_last[:, h1:h1 + 1])
        xdt_p = xs_p * dt_p
        xdt16 = xdt_p.astype(BF16)
        xdec_scr[g, :, pr * LANES:(pr + 1) * LANES] = (xdt_p * jnp.exp2(csl_p - cs_p)).astype(BF16)
        cd_parts.append(jnp.exp2(csl_p))
        eo_parts.append(jnp.exp2(cs_p))
        ys = []
        for h in (h0, h1):
            diff = cs[:, h:h + 1] - cs_t[h:h + 1, :]
            lmat = jnp.exp2(jnp.where(causal, diff, -jnp.inf))
            ys.append(_dot((cb * lmat).astype(BF16), xdt16))
        ydiag_scr[g, :, pr * LANES:(pr + 1) * LANES] = jnp.where(lo_half, ys[0], ys[1])

    eo = jnp.concatenate(eo_parts, axis=-1)
    cd = jnp.concatenate(cd_parts, axis=-1)
    state = state_scr[g]
    y = ydiag_scr[g] + eo * _dot(cm16, state.astype(BF16)) + xs * dskip_ref[:, gcols]
    state_scr[g] = state * cd + _dot(bm16.astype(F32).T.astype(BF16), xdec_scr[g])

    y = y * sz_ref[0, :, gcols]
    y = y * lax.rsqrt(jnp.mean(y * y, axis=-1, keepdims=True) + RMS_EPS) * nw_ref[:, gcols]
    y_ref[0, :, gcols] = y.astype(y_ref.dtype)


def _ssd(xs, bc, dt, sz, dtb, alog, dskip, norm_w):
    bsz, s, d_ssm = xs.shape
    L = SSD_CHUNK
    nc = s // L
    G = N_SSM_GROUPS
    gw = d_ssm // G

    def seq(width):
        return pl.BlockSpec((1, L, width), lambda b, c: (b, c, 0))

    def par(width):
        return pl.BlockSpec((1, width), lambda b, c: (0, 0))

    in_specs = [seq(d_ssm), seq(bc.shape[-1]), seq(G * LANES), seq(d_ssm),
                par(G * LANES), par(G * LANES), par(d_ssm), par(d_ssm)]
    return pl.pallas_call(
        _ssd_kernel,
        grid=(bsz, nc),
        in_specs=in_specs,
        out_specs=seq(d_ssm),
        out_shape=jax.ShapeDtypeStruct((bsz, s, d_ssm), BF16),
        scratch_shapes=[pltpu.VMEM((G, D_STATE, gw), F32), pltpu.VMEM((G, L, gw), BF16),
                        pltpu.VMEM((G, L, gw), F32)],
        compiler_params=_cparams(("parallel", "arbitrary")),
        name="ssd",
    )(xs, bc, dt, sz, dtb, alog, dskip, norm_w)


def _outproj_kernel(oa_ref, y_ref, x_ref, mod_ref, w_ref, g_ref, b_ref, o_ref, *, alpha, d_attn):
    gate = mod_ref[0, 2:3, :]
    tm = x_ref.shape[1]
    half = tm // 2
    for r0 in range(0, tm, half):
        rows = slice(r0, r0 + half)
        acc = _dot(oa_ref[0, rows, :], w_ref[0:d_attn, :]) + _dot(y_ref[0, rows, :], w_ref[d_attn:, :])
        r = alpha * x_ref[0, rows, :] + (1.0 + gate) * acc
        o_ref[0, rows, :] = _layer_norm(r, g_ref[...], b_ref[...])


def _outproj(oa, y, x, mod3, w, g, b, alpha, tm=512):
    bsz, s, d = x.shape
    d_attn = oa.shape[-1]
    d_ssm = y.shape[-1]
    kern = functools.partial(_outproj_kernel, alpha=alpha, d_attn=d_attn)
    return pl.pallas_call(
        kern,
        grid=(bsz, s // tm),
        in_specs=[pl.BlockSpec((1, tm, d_attn), lambda b, i: (b, i, 0)),
                  pl.BlockSpec((1, tm, d_ssm), lambda b, i: (b, i, 0)),
                  pl.BlockSpec((1, tm, d), lambda b, i: (b, i, 0)),
                  pl.BlockSpec((1, 6, d), lambda b, i: (b, 0, 0)),
                  pl.BlockSpec((d_attn + d_ssm, d), lambda b, i: (0, 0), pipeline_mode=pl.Buffered(1)),
                  pl.BlockSpec((1, d), lambda b, i: (0, 0)),
                  pl.BlockSpec((1, d), lambda b, i: (0, 0))],
        out_specs=pl.BlockSpec((1, tm, d), lambda b, i: (b, i, 0)),
        out_shape=jax.ShapeDtypeStruct((bsz, s, d), F32),
        compiler_params=_cparams(("parallel", "parallel")),
        name="outproj_ln",
    )(oa, y, x, mod3, w, g.reshape(1, d), b.reshape(1, d))


def _ffn_kernel(x_ref, mod_ref, wu_ref, cw_ref, cb_ref, wd_ref, g_ref, b_ref, o_ref,
                h_scr, halo_scr, act_scr, *, alpha, d_ff, chunk):
    tm = x_ref.shape[1]
    taps = cw_ref.shape[0]
    first = pl.program_id(1) == 0

    @pl.when(first)
    def _():
        halo_scr[...] = jnp.zeros_like(halo_scr)

    x = x_ref[0]
    sh = mod_ref[0, 3:4, :]
    sc = mod_ref[0, 4:5, :]
    h_scr[...] = (x * (1.0 + sc) + sh).astype(BF16)

    def conv_cols(c0):
        u = _dot(h_scr[...], wu_ref[:, c0:c0 + chunk])
        ext = jnp.concatenate([halo_scr[:, c0:c0 + chunk], u], axis=0)
        halo_scr[:, c0:c0 + chunk] = u[tm - HALO:tm, :]
        return _causal_conv_rows(ext, cw_ref[:, c0:c0 + chunk]) + cb_ref[:, c0:c0 + chunk]

    for c0 in range(0, d_ff, chunk):
        gte = conv_cols(c0)
        val = conv_cols(d_ff + c0)
        act_scr[:, c0:c0 + chunk] = (gte * _sigmoid(gte) * val).astype(BF16)

    y = _dot(act_scr[...], wd_ref[...])
    gate = mod_ref[0, 5:6, :]
    r = alpha * x + (1.0 + gate) * y
    o_ref[0] = _layer_norm(r, g_ref[...], b_ref[...])


def _ffn(x, mod3, w_up, conv_w, conv_b, w_down, g, b, alpha, tm=256, chunk=256):
    bsz, s, d = x.shape
    d_ff = w_down.shape[0]
    taps = conv_w.shape[0]
    kern = functools.partial(_ffn_kernel, alpha=alpha, d_ff=d_ff, chunk=chunk)
    return pl.pallas_call(
        kern,
        grid=(bsz, s // tm),
        in_specs=[pl.BlockSpec((1, tm, d), lambda b, i: (b, i, 0)),
                  pl.BlockSpec((1, 6, d), lambda b, i: (b, 0, 0)),
                  pl.BlockSpec((d, 2 * d_ff), lambda b, i: (0, 0), pipeline_mode=pl.Buffered(1)),
                  pl.BlockSpec((taps, 2 * d_ff), lambda b, i: (0, 0)),
                  pl.BlockSpec((1, 2 * d_ff), lambda b, i: (0, 0)),
                  pl.BlockSpec((d_ff, d), lambda b, i: (0, 0), pipeline_mode=pl.Buffered(1)),
                  pl.BlockSpec((1, d), lambda b, i: (0, 0)),
                  pl.BlockSpec((1, d), lambda b, i: (0, 0))],
        out_specs=pl.BlockSpec((1, tm, d), lambda b, i: (b, i, 0)),
        out_shape=jax.ShapeDtypeStruct((bsz, s, d), F32),
        scratch_shapes=[pltpu.VMEM((tm, d), BF16),
                        pltpu.VMEM((HALO, 2 * d_ff), F32),
                        pltpu.VMEM((tm, d_ff), BF16)],
        compiler_params=_cparams(("parallel", "arbitrary")),
        name="ffn_ln",
    )(x, mod3, w_up, conv_w, conv_b.reshape(1, 2 * d_ff), w_down, g.reshape(1, d), b.reshape(1, d))


def _pad_cols(a, width):
    return jnp.pad(a, ((0, 0), (0, width - a.shape[1])))


def kernel(x, c, ada_w, ada_b, mix_in_w, ssm_conv_w, ssm_conv_b, ssm_dt_bias, ssm_a_log, ssm_d, ssm_norm_w,
           mix_out_w, ln1_g, ln1_b, ffn_up_w, ffn_conv_w, ffn_conv_b, ffn_down_w, ln2_g, ln2_b):
    depth = ada_w.shape[0]
    bsz, s, d = x.shape
    assert s % max(MOBA_BLOCK, SSD_CHUNK) == 0
    d_attn = N_ATTN_HEADS * HEAD_DIM
    d_ssm = N_SSM_HEADS * SSM_HEAD_DIM
    d_xbc = d_ssm + 2 * N_SSM_GROUPS * D_STATE
    hpg = N_SSM_HEADS // N_SSM_GROUPS
    alpha = (2.0 * depth) ** 0.25

    slope_vals = (2.0 ** (-8.0 * np.arange(1, N_ATTN_HEADS + 1) / N_ATTN_HEADS)).astype(np.float32)
    slopes = jnp.asarray(np.broadcast_to(slope_vals.reshape(N_ATTN_HEADS // 2, 2, 1),
                                         (N_ATTN_HEADS // 2, 2, ATTN_Q_BLOCKS * MOBA_BLOCK)))
    key_tab = _attn_key_table(slope_vals, s)

    def per_group_lanes(v):
        return _pad_cols(v.reshape(N_SSM_GROUPS, hpg), LANES).reshape(1, N_SSM_GROUPS * LANES)

    for li in range(depth):
        mod3 = _adaln(c, ada_w[li], ada_b[li]).reshape(bsz, 6, d)

        w_in = mix_in_w[li]
        n_main = 3 * d_attn + d_ssm + d_xbc
        w_dt = w_in[:, n_main:].reshape(d, N_SSM_GROUPS, hpg)
        w_dt = jnp.pad(w_dt, ((0, 0), (0, 0), (0, LANES - hpg))).reshape(d, N_SSM_GROUPS * LANES).astype(BF16)
        w_t = jnp.swapaxes(w_in, 0, 1).astype(BF16)
        segs = ((0, 3 * d_attn), (3 * d_attn, d_ssm), (3 * d_attn + d_ssm, d_xbc))
        qkv, sz, xs, bc, dt = _inproj(x, mod3, w_t, w_dt, ssm_conv_w[li], ssm_conv_b[li].reshape(1, d_xbc),
                                      segs, d_ssm)

        o_attn = _attention(qkv, key_tab, slopes, ATTN_Q_BLOCKS)
        y_ssm = _ssd(xs, bc, dt, sz, per_group_lanes(ssm_dt_bias[li]), per_group_lanes(ssm_a_log[li]),
                     jnp.repeat(ssm_d[li], SSM_HEAD_DIM).reshape(1, d_ssm), ssm_norm_w[li].reshape(1, d_ssm))

        x = _outproj(o_attn, y_ssm, x, mod3, mix_out_w[li].astype(BF16), ln1_g[li], ln1_b[li], alpha)
        x = _ffn(x, mod3, ffn_up_w[li].astype(BF16), ffn_conv_w[li], ffn_conv_b[li],
                 ffn_down_w[li].astype(BF16), ln2_g[li], ln2_b[li], alpha)
    return x
```
